```python
import jax, jax.numpy as jnp
from jax import lax
import numpy as np

D_MODEL = 1024
BATCH = 8
SEQ = 8192
DEPTH = 1
DEC_BATCH = 2
DEC_SEQ = 8192
PAST_LEN = 128

GRID_W = 64
HEAD_DIM = 64
NA_HEADS = 8
NA_WIDTH = NA_HEADS * HEAD_DIM
NA_KH_MAX = 8
NA_KW = 16
SW_HEADS = 8
SW_KV_HEADS = 2
SW_GROUP = SW_HEADS // SW_KV_HEADS
SW_WIDTH = SW_HEADS * HEAD_DIM
SW_KV_WIDTH = SW_KV_HEADS * HEAD_DIM
SW_WINDOW = 128
SW_BLOCK = 128
RMS_EPS = 1e-6
IN_SIZES = (NA_WIDTH, NA_WIDTH, NA_WIDTH, NA_WIDTH,
            SW_WIDTH, SW_KV_WIDTH, SW_KV_WIDTH, SW_WIDTH,
            D_MODEL, D_MODEL)
D_IN = 4 * NA_WIDTH + 2 * SW_WIDTH + 2 * SW_KV_WIDTH + 2 * D_MODEL

kernel_name = "hybrid_natten_swa_gated_encoder"


def rms_norm(x, g):
    xf = x.astype(jnp.float32)
    y = xf * lax.rsqrt(jnp.mean(xf * xf, axis=-1, keepdims=True) + RMS_EPS)
    return (y * g.astype(jnp.float32)).astype(x.dtype)


def alibi_slopes(n):
    return jnp.asarray(2.0 ** (-8.0 * (np.arange(n) + 1) / n), dtype=jnp.float32)


def neighbourhood_attention(q, k, v, rpb):
    B, S, H, Dh = q.shape
    rows = S // GRID_W
    kh = min(NA_KH_MAX, rows)
    scale = Dh ** -0.5
    qg = q.reshape(B, rows, GRID_W, H, Dh)
    kg = k.reshape(B, rows, GRID_W, H, Dh)
    vg = v.reshape(B, rows, GRID_W, H, Dh)
    col = np.arange(GRID_W)
    col_start = np.clip(col - NA_KW // 2, 0, GRID_W - NA_KW)
    col_idx = col_start[:, None] + np.arange(NA_KW)[None, :]
    dc = col_idx - col[:, None] + NA_KW - 1
    rpb_c = rpb.astype(jnp.float32)[:, :, dc]

    def one_row(r):
        r0 = jnp.clip(r - kh // 2, 0, rows - kh)
        k_rows = lax.dynamic_slice_in_dim(kg, r0, kh, axis=1)
        v_rows = lax.dynamic_slice_in_dim(vg, r0, kh, axis=1)
        k_win = k_rows[:, :, col_idx]
        v_win = v_rows[:, :, col_idx]
        q_row = lax.dynamic_index_in_dim(qg, r, axis=1, keepdims=False)
        s = jnp.einsum('bwhd,biwjhd->bhwij', q_row, k_win,
                       preferred_element_type=jnp.float32) * scale
        dr = r0 + jnp.arange(kh) - r + NA_KH_MAX - 1
        bias = jnp.take(rpb_c, dr, axis=1).transpose(0, 2, 1, 3)
        s = s + bias[None]
        p = jax.nn.softmax(s.reshape(B, H, GRID_W, kh * NA_KW), axis=-1)
        p = p.reshape(B, H, GRID_W, kh, NA_KW).astype(v.dtype)
        return jnp.einsum('bhwij,biwjhd->bwhd', p, v_win)

    out = lax.map(one_row, jnp.arange(rows))
    return out.transpose(1, 0, 2, 3, 4).reshape(B, S, H, Dh)


def sliding_window_attention(q, k, v, sink):
    B, S, H, Dh = q.shape
    kv = k.shape[2]
    g = H // kv
    nb = S // SW_BLOCK
    scale = Dh ** -0.5
    qb = q.reshape(B, nb, SW_BLOCK, kv, g, Dh)
    pad = ((0, 0), (SW_BLOCK, SW_BLOCK), (0, 0), (0, 0))
    kp = jnp.pad(k, pad).reshape(B, nb + 2, SW_BLOCK, kv, Dh)
    vp = jnp.pad(v, pad).reshape(B, nb + 2, SW_BLOCK, kv, Dh)
    kb = jnp.concatenate([kp[:, :-2], kp[:, 1:-1], kp[:, 2:]], axis=2)
    vb = jnp.concatenate([vp[:, :-2], vp[:, 1:-1], vp[:, 2:]], axis=2)
    q_pos = jnp.arange(SW_BLOCK)
    k_pos = jnp.arange(3 * SW_BLOCK) - SW_BLOCK
    dist = jnp.abs(k_pos[None, :] - q_pos[:, None])
    abs_k = jnp.arange(nb)[:, None] * SW_BLOCK + k_pos[None, :]
    valid = (dist[None] <= SW_WINDOW) & ((abs_k >= 0) & (abs_k < S))[:, None, :]
    slopes = alibi_slopes(H).reshape(kv, g)
    s = jnp.einsum('bnqkgd,bnskd->bnkgqs', qb, kb,
                   preferred_element_type=jnp.float32) * scale
    s = s - slopes[:, :, None, None] * dist.astype(jnp.float32)
    s = jnp.where(valid[None, :, None, None], s, -jnp.inf)
    sink_kg = sink.astype(jnp.float32).reshape(kv, g)[None, None, :, :, None, None]
    m = jnp.maximum(jnp.max(s, axis=-1, keepdims=True), sink_kg)
    e = jnp.exp(s - m)
    p = e / (jnp.sum(e, axis=-1, keepdims=True) + jnp.exp(sink_kg - m))
    o = jnp.einsum('bnkgqs,bnskd->bnqkgd', p.astype(v.dtype), vb)
    return o.reshape(B, S, H, Dh)


def encoder_layer(x, c, w_ada, b_ada, g_pre, g_post, w_in, na_rpb, sw_sink, w_pa, w_pb, w_out):
    B, S, D = x.shape
    ada = jnp.einsum('bd,de->be', jax.nn.silu(c), w_ada) + b_ada
    shift, scale, gate = jnp.split(ada[:, None, :], 3, axis=-1)
    h = rms_norm(x, g_pre) * (1 + scale) + shift
    proj = jnp.einsum('bsd,de->bse', h, w_in)
    cuts = [int(v) for v in np.cumsum(IN_SIZES)[:-1]]
    qa, ka, va, za, qb, kb, vb, zb, ga, gb = jnp.split(proj, cuts, axis=-1)
    oa = neighbourhood_attention(qa.reshape(B, S, NA_HEADS, HEAD_DIM),
                                 ka.reshape(B, S, NA_HEADS, HEAD_DIM),
                                 va.reshape(B, S, NA_HEADS, HEAD_DIM), na_rpb)
    oa = oa.reshape(B, S, NA_WIDTH) * jax.nn.silu(za)
    ob = sliding_window_attention(qb.reshape(B, S, SW_HEADS, HEAD_DIM),
                                  kb.reshape(B, S, SW_KV_HEADS, HEAD_DIM),
                                  vb.reshape(B, S, SW_KV_HEADS, HEAD_DIM), sw_sink)
    ob = ob.reshape(B, S, SW_WIDTH) * jax.nn.silu(zb)
    merged = (jax.nn.sigmoid(ga) * jnp.einsum('bse,ed->bsd', oa, w_pa)
              + jax.nn.sigmoid(gb) * jnp.einsum('bse,ed->bsd', ob, w_pb))
    y = jnp.einsum('bsd,de->bse', merged, w_out)
    return x + gate * rms_norm(y, g_post)


def setup_inputs(seed: int = 0) -> dict:
    key = jax.random.key(seed)
    ks = jax.random.split(key, 16)
    f32 = jnp.float32

    def nrm(k, shape, s):
        return jax.random.normal(k, shape, dtype=f32) * s

    return {
        "x_prompt": nrm(ks[0], (BATCH, SEQ, D_MODEL), 1.0),
        "x_sample": nrm(ks[1], (DEC_BATCH, DEC_SEQ, D_MODEL), 1.0),
        "c_prompt": nrm(ks[2], (BATCH, D_MODEL), 1.0),
        "c_sample": nrm(ks[3], (DEC_BATCH, D_MODEL), 1.0),
        "w_ada": nrm(ks[4], (DEPTH, D_MODEL, 3 * D_MODEL), 0.5 * D_MODEL ** -0.5),
        "b_ada": nrm(ks[5], (DEPTH, 3 * D_MODEL), 0.01),
        "g_pre": 1.0 + nrm(ks[6], (DEPTH, D_MODEL), 0.01),
        "g_post": 1.0 + nrm(ks[7], (DEPTH, D_MODEL), 0.01),
        "w_in": nrm(ks[8], (DEPTH, D_MODEL, D_IN), D_MODEL ** -0.5),
        "na_rpb": nrm(ks[9], (DEPTH, NA_HEADS, 2 * NA_KH_MAX - 1, 2 * NA_KW - 1), 0.1),
        "sw_sink": nrm(ks[10], (DEPTH, SW_HEADS), 0.5),
        "w_pa": nrm(ks[11], (DEPTH, NA_WIDTH, D_MODEL), NA_WIDTH ** -0.5),
        "w_pb": nrm(ks[12], (DEPTH, SW_WIDTH, D_MODEL), SW_WIDTH ** -0.5),
        "w_out": nrm(ks[13], (DEPTH, D_MODEL, D_MODEL), D_MODEL ** -0.5),
    }


def reference(x_prompt, x_sample, c_prompt, c_sample, w_ada, b_ada, g_pre, g_post,
              w_in, na_rpb, sw_sink, w_pa, w_pb, w_out):
    y_prompt = x_prompt
    y_sample = x_sample
    for l in range(DEPTH):
        y_prompt = encoder_layer(y_prompt, c_prompt, w_ada[l], b_ada[l], g_pre[l], g_post[l],
                                 w_in[l], na_rpb[l], sw_sink[l], w_pa[l], w_pb[l], w_out[l])
        y_sample = encoder_layer(y_sample, c_sample, w_ada[l], b_ada[l], g_pre[l], g_post[l],
                                 w_in[l], na_rpb[l], sw_sink[l], w_pa[l], w_pb[l], w_out[l])
    return (y_prompt, y_sample)
```

```python
import functools

import jax
import jax.numpy as jnp
import numpy as np
from jax import lax
from jax.experimental import pallas as pl
from jax.experimental.pallas import tpu as pltpu

D_MODEL = 1024
GRID_W = 64
HEAD_DIM = 64
NA_HEADS = 8
NA_WIDTH = NA_HEADS * HEAD_DIM
NA_KH = 8
NA_KH_MAX = 8
NA_KW = 16
SW_HEADS = 8
SW_KV_HEADS = 2
SW_GROUP = SW_HEADS // SW_KV_HEADS
SW_WIDTH = SW_HEADS * HEAD_DIM
SW_KV_WIDTH = SW_KV_HEADS * HEAD_DIM
SW_WINDOW = 128
SW_BLOCK = 128
RMS_EPS = 1e-6
D_IN = 4 * NA_WIDTH + 2 * SW_WIDTH + 2 * SW_KV_WIDTH + 2 * D_MODEL

LANES = 128
NEG = -1e30
VMEM_LIMIT = 56 * 1024 * 1024

OFF_GA, OFF_GB = 0, D_MODEL
OFF_QA = 2 * D_MODEL
OFF_KA = OFF_QA + NA_WIDTH
OFF_VA = OFF_KA + NA_WIDTH
OFF_ZA = OFF_VA + NA_WIDTH
OFF_QB = OFF_ZA + NA_WIDTH
OFF_ZB = OFF_QB + SW_WIDTH
OFF_KB = OFF_ZB + SW_WIDTH
OFF_VB = OFF_KB + SW_KV_WIDTH
SW_HEAD_ORDER = (0, 4, 1, 5, 2, 6, 3, 7)

NA_QROWS = 4
NA_KROWS = 12
NA_QT = NA_QROWS * GRID_W
NA_KT = NA_KROWS * GRID_W
SW_KT = 3 * SW_BLOCK

TM_IN = 512
CN_IN = 768
TM_OUT = 512
SW_CHUNK = 2048


def _in_col_perm():
    o = np.cumsum((0, NA_WIDTH, NA_WIDTH, NA_WIDTH, NA_WIDTH, SW_WIDTH, SW_KV_WIDTH, SW_KV_WIDTH, SW_WIDTH,
                   D_MODEL))
    qa, ka, va, za, qb, kb, vb, zb, ga, gb = (int(v) for v in o)
    ar = np.arange
    heads = np.concatenate([h * HEAD_DIM + ar(HEAD_DIM) for h in SW_HEAD_ORDER])
    return np.concatenate([
        ga + ar(D_MODEL), gb + ar(D_MODEL),
        qa + ar(NA_WIDTH), ka + ar(NA_WIDTH), va + ar(NA_WIDTH), za + ar(NA_WIDTH),
        qb + heads, zb + heads, kb + ar(SW_KV_WIDTH), vb + ar(SW_KV_WIDTH)]), heads


IN_COL_PERM, SW_HEAD_PERM = _in_col_perm()


def _ada_kernel(c_ref, w_ref, b_ref, o_ref):
    c = c_ref[...]
    sc = c * jax.nn.sigmoid(c)
    o_ref[...] = jnp.dot(sc, w_ref[...], preferred_element_type=jnp.float32,
                         precision=lax.Precision.HIGHEST) + b_ref[...]


def _ada(c, w_ada, b_ada):
    nb, d = c.shape
    n = w_ada.shape[1]
    tn = 1024
    return pl.pallas_call(
        _ada_kernel,
        grid=(n // tn,),
        in_specs=[pl.BlockSpec((nb, d), lambda j: (0, 0)),
                  pl.BlockSpec((d, tn), lambda j: (0, j)),
                  pl.BlockSpec((1, tn), lambda j: (0, j))],
        out_specs=pl.BlockSpec((nb, tn), lambda j: (0, j)),
        out_shape=jax.ShapeDtypeStruct((nb, n), jnp.float32),
        name="ada",
    )(c, w_ada, b_ada.reshape(1, n))


def _in_proj_kernel(x_ref, scale_ref, shift_ref, g_ref, w_ref, o_ref):
    x = x_ref[0]
    ms = jnp.mean(x * x, axis=-1, keepdims=True)
    y = x * lax.rsqrt(ms + RMS_EPS) * g_ref[...]
    h = (y * (1.0 + scale_ref[0]) + shift_ref[0]).astype(jnp.bfloat16)
    for c in range(D_IN // CN_IN):
        cols = slice(c * CN_IN, (c + 1) * CN_IN)
        o_ref[0, :, cols] = jnp.dot(h, w_ref[:, cols],
                                    preferred_element_type=jnp.float32).astype(jnp.bfloat16)


def _in_proj(x, scale, shift, g_pre, w_in):
    b, s, d = x.shape
    return pl.pallas_call(
        _in_proj_kernel,
        grid=(b, s // TM_IN),
        in_specs=[pl.BlockSpec((1, TM_IN, d), lambda i, j: (i, j, 0)),
                  pl.BlockSpec((1, 1, d), lambda i, j: (i, 0, 0)),
                  pl.BlockSpec((1, 1, d), lambda i, j: (i, 0, 0)),
                  pl.BlockSpec((1, d), lambda i, j: (0, 0)),
                  pl.BlockSpec((d, D_IN), lambda i, j: (0, 0))],
        out_specs=pl.BlockSpec((1, TM_IN, D_IN), lambda i, j: (i, j, 0)),
        out_shape=jax.ShapeDtypeStruct((b, s, D_IN), jnp.bfloat16),
        compiler_params=pltpu.CompilerParams(
            dimension_semantics=("parallel", "parallel"), vmem_limit_bytes=VMEM_LIMIT),
        name="in_proj",
    )(x, scale, shift, g_pre, w_in)


def _na_bias_tables(rpb, rows):
    c = np.arange(GRID_W)
    c0 = np.clip(c - NA_KW // 2, 0, GRID_W - NA_KW)
    kc = np.arange(GRID_W)
    col_ok = (kc[None, :] >= c0[:, None]) & (kc[None, :] < c0[:, None] + NA_KW)
    dc = np.clip(kc[None, :] - c[:, None] + NA_KW - 1, 0, 2 * NA_KW - 2)
    a = np.arange(NA_QROWS)
    j = np.arange(NA_KROWS)
    idx_r, ok = [], []
    for r, ks in ((0, 0), (NA_QROWS, 0), (rows - NA_QROWS, rows - NA_KROWS)):
        rq = r + a
        r0 = np.clip(rq - NA_KH // 2, 0, rows - NA_KH)
        kr = ks + j
        row_ok = (kr[None, :] >= r0[:, None]) & (kr[None, :] < r0[:, None] + NA_KH)
        dr = np.clip(kr[None, :] - rq[:, None] + NA_KH_MAX - 1, 0, 2 * NA_KH_MAX - 2)
        idx_r.append(np.broadcast_to(dr[:, None, :, None], (NA_QROWS, GRID_W, NA_KROWS, GRID_W)))
        ok.append(row_ok[:, None, :, None] & col_ok[None, :, None, :])
    idx_r = np.stack(idx_r).reshape(3, NA_QT, NA_KT)
    ok = np.stack(ok).reshape(3, NA_QT, NA_KT)
    idx_c = np.broadcast_to(dc[None, None, :, None, :],
                            (3, NA_QROWS, GRID_W, NA_KROWS, GRID_W)).reshape(3, NA_QT, NA_KT)
    t = rpb.astype(jnp.float32)[:, idx_r, idx_c]
    t = jnp.where(ok[None], t, NEG)
    return t.reshape(NA_HEADS // 2, 2, 3, NA_QT, NA_KT).transpose(0, 2, 1, 3, 4)


def _na_kernel(q_ref, k_ref, v_ref, z_ref, bias_ref, o_ref, *, rows):
    nblk = rows // NA_QROWS
    lane = lax.broadcasted_iota(jnp.int32, (1, LANES), 1)
    low = lane < HEAD_DIM

    def body(rb, carry):
        ks = jnp.clip(rb * NA_QROWS - NA_KH // 2, 0, rows - NA_KROWS)
        variant = jnp.where(rb == 0, 0, jnp.where(rb == nblk - 1, 2, 1))
        q0 = pl.multiple_of(rb * NA_QT, NA_QT)
        k0 = pl.multiple_of(ks * GRID_W, GRID_W)
        q = q_ref[0, pl.ds(q0, NA_QT), :] * jnp.bfloat16(HEAD_DIM ** -0.5)
        k = k_ref[0, pl.ds(k0, NA_KT), :]
        v = v_ref[0, pl.ds(k0, NA_KT), :]
        outs = []
        for hh in range(2):
            qm = jnp.where(low if hh == 0 else jnp.logical_not(low), q, jnp.zeros_like(q))
            s = lax.dot_general(qm, k, (((1,), (1,)), ((), ())),
                                preferred_element_type=jnp.float32)
            s = s + bias_ref[0, variant, hh]
            m = jnp.max(s, axis=-1, keepdims=True)
            e = jnp.exp(s - m)
            l = jnp.sum(e, axis=-1, keepdims=True)
            pv = jnp.dot(e.astype(jnp.bfloat16), v, preferred_element_type=jnp.float32)
            outs.append(pv * (1.0 / l))
        o = jnp.where(low, outs[0], outs[1])
        z = z_ref[0, pl.ds(q0, NA_QT), :].astype(jnp.float32)
        o_ref[0, pl.ds(q0, NA_QT), :] = (o * (z * jax.nn.sigmoid(z))).astype(jnp.bfloat16)
        return carry

    lax.fori_loop(0, nblk, body, 0)


def _na_attn(proj, bias):
    b, s, _ = proj.shape
    rows = s // GRID_W
    npair = NA_HEADS // 2

    def col(off):
        return lambda i, p: (i, 0, off // LANES + p)

    return pl.pallas_call(
        functools.partial(_na_kernel, rows=rows),
        grid=(b, npair),
        in_specs=[pl.BlockSpec((1, s, LANES), col(OFF_QA)),
                  pl.BlockSpec((1, s, LANES), col(OFF_KA)),
                  pl.BlockSpec((1, s, LANES), col(OFF_VA)),
                  pl.BlockSpec((1, s, LANES), col(OFF_ZA)),
                  pl.BlockSpec((1, 3, 2, NA_QT, NA_KT), lambda i, p: (p, 0, 0, 0, 0))],
        out_specs=pl.BlockSpec((1, s, LANES), lambda i, p: (i, 0, p)),
        out_shape=jax.ShapeDtypeStruct((b, s, NA_WIDTH), jnp.bfloat16),
        compiler_params=pltpu.CompilerParams(
            dimension_semantics=("parallel", "parallel"), vmem_limit_bytes=VMEM_LIMIT),
        name="na_attn",
    )(proj, proj, proj, proj, bias)


def _sw_bias_tables():
    slopes = 2.0 ** (-8.0 * (np.arange(SW_HEADS) + 1) / SW_HEADS)
    i = np.arange(SW_BLOCK)
    col = np.arange(SW_KT)
    out = np.empty((3, SW_KV_HEADS, SW_GROUP, SW_BLOCK, SW_KT), np.float32)
    for var in range(3):
        dist = np.abs(col[None, :] - var * SW_BLOCK - i[:, None])
        for g in range(SW_KV_HEADS):
            for jj in range(SW_GROUP):
                bias = -(slopes[g * SW_GROUP + jj].astype(np.float32) * dist.astype(np.float32))
                out[var, g, jj] = np.where(dist <= SW_WINDOW, bias, NEG)
    return out.reshape(3, SW_KV_HEADS, SW_GROUP * SW_BLOCK, SW_KT)


def _sw_kernel(q_ref, k_ref, v_ref, z_ref, bias_ref, sink_ref, o_ref, *, seq):
    nblk_total = seq // SW_BLOCK
    nblk = SW_CHUNK // SW_BLOCK
    chunk = pl.program_id(1)
    lane = lax.broadcasted_iota(jnp.int32, (1, LANES), 1)
    low = lane < HEAD_DIM

    def body(i, carry):
        nb = chunk * nblk + i
        start = jnp.clip((nb - 1) * SW_BLOCK, 0, seq - SW_KT)
        variant = jnp.where(nb == 0, 0, jnp.where(nb == nblk_total - 1, 2, 1))
        q0 = pl.multiple_of(i * SW_BLOCK, SW_BLOCK)
        k0 = pl.multiple_of(start, SW_BLOCK)
        qb = q_ref[0, pl.ds(q0, SW_BLOCK), :] * jnp.bfloat16(HEAD_DIM ** -0.5)
        lhs = jnp.concatenate([qb[:, jj * LANES:(jj + 1) * LANES] for jj in range(SW_GROUP)], axis=0)
        k = k_ref[0, pl.ds(k0, SW_KT), :]
        v = v_ref[0, pl.ds(k0, SW_KT), :]
        outs = []
        for g in range(SW_KV_HEADS):
            kg = jnp.where(low if g == 0 else jnp.logical_not(low), k, jnp.zeros_like(k))
            s = lax.dot_general(lhs, kg, (((1,), (1,)), ((), ())),
                                preferred_element_type=jnp.float32)
            s = s + bias_ref[variant, g]
            sink = sink_ref[g]
            m = jnp.maximum(jnp.max(s, axis=-1, keepdims=True), sink)
            e = jnp.exp(s - m)
            l = jnp.sum(e, axis=-1, keepdims=True) + jnp.exp(sink - m)
            pv = jnp.dot(e.astype(jnp.bfloat16), v, preferred_element_type=jnp.float32)
            outs.append(pv * (1.0 / l))
        o = jnp.where(low, outs[0], outs[1])
        o = jnp.concatenate([o[jj * SW_BLOCK:(jj + 1) * SW_BLOCK] for jj in range(SW_GROUP)], axis=1)
        z = z_ref[0, pl.ds(q0, SW_BLOCK), :].astype(jnp.float32)
        o_ref[0, pl.ds(q0, SW_BLOCK), :] = (o * (z * jax.nn.sigmoid(z))).astype(jnp.bfloat16)
        return carry

    lax.fori_loop(0, nblk, body, 0)


def _sw_attn(proj, bias, sink):
    b, s, _ = proj.shape
    return pl.pallas_call(
        functools.partial(_sw_kernel, seq=s),
        grid=(b, s // SW_CHUNK),
        in_specs=[pl.BlockSpec((1, SW_CHUNK, SW_WIDTH), lambda i, c: (i, c, OFF_QB // SW_WIDTH)),
                  pl.BlockSpec((1, s, LANES), lambda i, c: (i, 0, OFF_KB // LANES)),
                  pl.BlockSpec((1, s, LANES), lambda i, c: (i, 0, OFF_VB // LANES)),
                  pl.BlockSpec((1, SW_CHUNK, SW_WIDTH), lambda i, c: (i, c, OFF_ZB // SW_WIDTH)),
                  pl.BlockSpec(bias.shape, lambda i, c: (0, 0, 0, 0)),
                  pl.BlockSpec(sink.shape, lambda i, c: (0, 0, 0))],
        out_specs=pl.BlockSpec((1, SW_CHUNK, SW_WIDTH), lambda i, c: (i, c, 0)),
        out_shape=jax.ShapeDtypeStruct((b, s, SW_WIDTH), jnp.bfloat16),
        compiler_params=pltpu.CompilerParams(
            dimension_semantics=("parallel", "parallel"), vmem_limit_bytes=VMEM_LIMIT),
        name="sw_attn",
    )(proj, proj, proj, proj, bias, sink)


def _out_kernel(x_ref, oa_ref, ob_ref, ga_ref, gb_ref, gate_ref, g_ref, wpa_ref, wpb_ref, wout_ref, o_ref):
    a = jnp.dot(oa_ref[0], wpa_ref[...], preferred_element_type=jnp.float32)
    bb = jnp.dot(ob_ref[0], wpb_ref[...], preferred_element_type=jnp.float32)
    merged = (jax.nn.sigmoid(ga_ref[0].astype(jnp.float32)) * a
              + jax.nn.sigmoid(gb_ref[0].astype(jnp.float32)) * bb)
    y = jnp.dot(merged.astype(jnp.bfloat16), wout_ref[...], preferred_element_type=jnp.float32)
    ms = jnp.mean(y * y, axis=-1, keepdims=True)
    yn = y * lax.rsqrt(ms + RMS_EPS) * g_ref[...]
    o_ref[0] = x_ref[0] + gate_ref[0] * yn


def _out_proj(x, oa, ob, proj, gate, g_post, w_pa, w_pb, w_out):
    b, s, d = x.shape
    tok = lambda i, j: (i, j, 0)
    const = lambda i, j: (0, 0)
    return pl.pallas_call(
        _out_kernel,
        grid=(b, s // TM_OUT),
        in_specs=[pl.BlockSpec((1, TM_OUT, d), tok),
                  pl.BlockSpec((1, TM_OUT, NA_WIDTH), tok),
                  pl.BlockSpec((1, TM_OUT, SW_WIDTH), tok),
                  pl.BlockSpec((1, TM_OUT, d), lambda i, j: (i, j, OFF_GA // D_MODEL)),
                  pl.BlockSpec((1, TM_OUT, d), lambda i, j: (i, j, OFF_GB // D_MODEL)),
                  pl.BlockSpec((1, 1, d), lambda i, j: (i, 0, 0)),
                  pl.BlockSpec((1, d), const),
                  pl.BlockSpec((NA_WIDTH, d), const),
                  pl.BlockSpec((SW_WIDTH, d), const),
                  pl.BlockSpec((d, d), const)],
        out_specs=pl.BlockSpec((1, TM_OUT, d), tok),
        out_shape=jax.ShapeDtypeStruct((b, s, d), jnp.float32),
        compiler_params=pltpu.CompilerParams(
            dimension_semantics=("parallel", "parallel"), vmem_limit_bytes=VMEM_LIMIT),
        name="out_proj",
    )(x, oa, ob, proj, proj, gate, g_post, w_pa, w_pb, w_out)


def _encoder_layer(xs, adas, g_pre, g_post, w_in, na_bias, sw_bias, sw_sink, w_pa, w_pb, w_out):
    outs = []
    for x, ada in zip(xs, adas):
        shift, scale, gate = jnp.split(ada[:, None, :], 3, axis=-1)
        proj = _in_proj(x, scale, shift, g_pre, w_in)
        oa = _na_attn(proj, na_bias)
        ob = _sw_attn(proj, sw_bias, sw_sink)
        outs.append(_out_proj(x, oa, ob, proj, gate, g_post, w_pa, w_pb, w_out))
    return outs


def kernel(x_prompt, x_sample, c_prompt, c_sample, w_ada, b_ada, g_pre, g_post, w_in, na_rpb, sw_sink,
           w_pa, w_pb, w_out):
    depth = w_in.shape[0]
    assert x_prompt.shape[1] % (GRID_W * NA_QROWS) == 0 and x_prompt.shape[1] % SW_CHUNK == 0
    assert x_sample.shape[1] == x_prompt.shape[1] and x_prompt.shape[1] // GRID_W >= 2 * NA_KROWS
    xs = [x_prompt, x_sample]
    nbs = [x_prompt.shape[0], x_sample.shape[0]]
    c_all = jnp.concatenate([c_prompt, c_sample], axis=0)
    sw_bias = jnp.asarray(_sw_bias_tables())
    bf16 = jnp.bfloat16
    for l in range(depth):
        ada = _ada(c_all, w_ada[l], b_ada[l])
        adas = [ada[:nbs[0]], ada[nbs[0]:]]
        w_in_l = w_in[l][:, IN_COL_PERM].astype(bf16)
        w_pb_l = w_pb[l][SW_HEAD_PERM].astype(bf16)
        na_bias = _na_bias_tables(na_rpb[l], xs[0].shape[1] // GRID_W)
        sink = sw_sink[l].astype(jnp.float32).reshape(SW_KV_HEADS, SW_GROUP, 1)
        sink = jnp.broadcast_to(sink[:, :, None, :], (SW_KV_HEADS, SW_GROUP, SW_BLOCK, 1))
        sink = sink.reshape(SW_KV_HEADS, SW_GROUP * SW_BLOCK, 1)
        xs = _encoder_layer(xs, adas, g_pre[l].reshape(1, -1), g_post[l].reshape(1, -1), w_in_l, na_bias,
                            sw_bias, sink, w_pa[l].astype(bf16), w_pb_l, w_out[l].astype(bf16))
    return (xs[0], xs[1])
```

```python
import functools
import math

import jax
import jax.numpy as jnp
import numpy as np
from jax import lax
from jax.experimental import pallas as pl
from jax.experimental.pallas import tpu as pltpu

D_MODEL = 1024
GRID_W = 64
HEAD_DIM = 64
NA_HEADS = 8
NA_WIDTH = NA_HEADS * HEAD_DIM
NA_KH = 8
NA_KH_MAX = 8
NA_KW = 16
SW_HEADS = 8
SW_KV_HEADS = 2
SW_GROUP = SW_HEADS // SW_KV_HEADS
SW_WIDTH = SW_HEADS * HEAD_DIM
SW_KV_WIDTH = SW_KV_HEADS * HEAD_DIM
SW_WINDOW = 128
SW_BLOCK = 128
RMS_EPS = 1e-6
D_IN = 4 * NA_WIDTH + 2 * SW_WIDTH + 2 * SW_KV_WIDTH + 2 * D_MODEL

LANES = 128
NEG = -1e30
LOG2E = math.log2(math.e)
VMEM_LIMIT = 56 * 1024 * 1024

OFF_GA, OFF_GB = 0, D_MODEL
OFF_QA = 2 * D_MODEL
OFF_KA = OFF_QA + NA_WIDTH
OFF_VA = OFF_KA + NA_WIDTH
OFF_ZA = OFF_VA + NA_WIDTH
OFF_QB = OFF_ZA + NA_WIDTH
OFF_ZB = OFF_QB + SW_WIDTH
OFF_KB = OFF_ZB + SW_WIDTH
OFF_VB = OFF_KB + SW_KV_WIDTH
SW_HEAD_ORDER = (0, 4, 1, 5, 2, 6, 3, 7)

NA_QROWS = 4
NA_KROWS = 12
NA_QT = NA_QROWS * GRID_W
NA_KT = NA_KROWS * GRID_W
SW_KT = 3 * SW_BLOCK

TM_IN = 512
CN_IN = 768
TM_OUT = 512
SW_CHUNK = 2048


def _in_col_perm():
    o = np.cumsum((0, NA_WIDTH, NA_WIDTH, NA_WIDTH, NA_WIDTH, SW_WIDTH, SW_KV_WIDTH, SW_KV_WIDTH, SW_WIDTH,
                   D_MODEL))
    qa, ka, va, za, qb, kb, vb, zb, ga, gb = (int(v) for v in o)
    ar = np.arange
    heads = np.concatenate([h * HEAD_DIM + ar(HEAD_DIM) for h in SW_HEAD_ORDER])
    perm = np.concatenate([
        ga + ar(D_MODEL), gb + ar(D_MODEL),
        qa + ar(NA_WIDTH), ka + ar(NA_WIDTH), va + ar(NA_WIDTH), za + ar(NA_WIDTH),
        qb + heads, zb + heads, kb + ar(SW_KV_WIDTH), vb + ar(SW_KV_WIDTH)])
    col_scale = np.ones((D_IN,), np.float32)
    col_scale[OFF_QA:OFF_QA + NA_WIDTH] = HEAD_DIM ** -0.5 * LOG2E
    col_scale[OFF_QB:OFF_QB + SW_WIDTH] = HEAD_DIM ** -0.5 * LOG2E
    return perm, col_scale, heads


IN_COL_PERM, IN_COL_SCALE, SW_HEAD_PERM = _in_col_perm()


def _low_lanes():
    return lax.broadcasted_iota(jnp.int32, (1, LANES), 1) < HEAD_DIM


def _ada_kernel(c_ref, w_ref, b_ref, o_ref):
    c = c_ref[...]
    sc = c * jax.nn.sigmoid(c)
    o_ref[...] = jnp.dot(sc, w_ref[...], preferred_element_type=jnp.float32,
                         precision=lax.Precision.HIGHEST) + b_ref[...]


def _ada(c, w_ada, b_ada):
    nb, d = c.shape
    n = w_ada.shape[1]
    tn = 1024
    return pl.pallas_call(
        _ada_kernel,
        grid=(n // tn,),
        in_specs=[pl.BlockSpec((nb, d), lambda j: (0, 0)),
                  pl.BlockSpec((d, tn), lambda j: (0, j)),
                  pl.BlockSpec((1, tn), lambda j: (0, j))],
        out_specs=pl.BlockSpec((nb, tn), lambda j: (0, j)),
        out_shape=jax.ShapeDtypeStruct((nb, n), jnp.float32),
        name="ada",
    )(c, w_ada, b_ada.reshape(1, n))


def _in_proj_kernel(x_ref, scale_ref, shift_ref, g_ref, w_ref, o_ref):
    x = x_ref[0]
    ms = jnp.mean(x * x, axis=-1, keepdims=True)
    y = x * lax.rsqrt(ms + RMS_EPS) * g_ref[...]
    h = (y * (1.0 + scale_ref[0]) + shift_ref[0]).astype(jnp.bfloat16)
    for c in range(D_IN // CN_IN):
        cols = slice(c * CN_IN, (c + 1) * CN_IN)
        o_ref[0, :, cols] = jnp.dot(h, w_ref[:, cols],
                                    preferred_element_type=jnp.float32).astype(jnp.bfloat16)


def _in_proj(x, scale, shift, g_pre, w_in):
    b, s, d = x.shape
    return pl.pallas_call(
        _in_proj_kernel,
        grid=(b, s // TM_IN),
        in_specs=[pl.BlockSpec((1, TM_IN, d), lambda i, j: (i, j, 0)),
                  pl.BlockSpec((1, 1, d), lambda i, j: (i, 0, 0)),
                  pl.BlockSpec((1, 1, d), lambda i, j: (i, 0, 0)),
                  pl.BlockSpec((1, d), lambda i, j: (0, 0)),
                  pl.BlockSpec((d, D_IN), lambda i, j: (0, 0))],
        out_specs=pl.BlockSpec((1, TM_IN, D_IN), lambda i, j: (i, j, 0)),
        out_shape=jax.ShapeDtypeStruct((b, s, D_IN), jnp.bfloat16),
        compiler_params=pltpu.CompilerParams(
            dimension_semantics=("parallel", "parallel"), vmem_limit_bytes=VMEM_LIMIT),
        name="in_proj",
    )(x, scale, shift, g_pre, w_in)


def _normalise_pair(pv_low, pv_high, low):
    val = jnp.where(low, pv_low, pv_high)
    den = pltpu.roll(jnp.where(low, pv_high, pv_low), HEAD_DIM, axis=1)
    return val, den


def _na_bias_tables(rpb, rows):
    pad = NA_QROWS
    c = np.arange(GRID_W)
    c0 = np.clip(c - NA_KW // 2, 0, GRID_W - NA_KW)
    col_ok = (c[None, :] >= c0[:, None]) & (c[None, :] < c0[:, None] + NA_KW)
    cpad = GRID_W - NA_KW
    rpb = jnp.pad(rpb.astype(jnp.float32) * LOG2E, ((0, 0), (0, 0), (cpad, cpad)))
    t = jnp.stack([rpb[:, :, GRID_W - 1 - ci:2 * GRID_W - 1 - ci] for ci in range(GRID_W)], axis=1)
    t = jnp.where(col_ok[None, :, None, :], t, NEG)
    t = jnp.pad(t, ((0, 0), (0, 0), (pad, pad), (0, 0)), constant_values=NEG)
    j = np.arange(NA_KROWS)
    blocks = []
    for r, ks in ((0, 0), (NA_QROWS, 0), (rows - NA_QROWS, rows - NA_KROWS)):
        strips = []
        for a in range(NA_QROWS):
            rq = r + a
            r0 = min(max(rq - NA_KH // 2, 0), rows - NA_KH)
            row_ok = (ks + j >= r0) & (ks + j < r0 + NA_KH)
            d0 = ks - rq + NA_KH_MAX - 1 + pad
            assert 0 <= d0 and d0 + NA_KROWS <= t.shape[2]
            strip = jnp.where(row_ok[None, None, :, None], t[:, :, d0:d0 + NA_KROWS, :], NEG)
            strips.append(strip.reshape(NA_HEADS, GRID_W, NA_KT))
        blocks.append(jnp.concatenate(strips, axis=1))
    t = jnp.stack(blocks, axis=1)
    return t.reshape(NA_HEADS // 2, 2, 3, NA_QT, NA_KT).transpose(0, 2, 1, 3, 4)


def _na_kernel(q_ref, k_ref, v_ref, z_ref, bias_ref, o_ref, *, rows):
    nblk = rows // NA_QROWS
    low = _low_lanes()
    high = jnp.logical_not(low)

    def body(rb, carry):
        ks = jnp.clip(rb * NA_QROWS - NA_KH // 2, 0, rows - NA_KROWS)
        variant = jnp.where(rb == 0, 0, jnp.where(rb == nblk - 1, 2, 1))
        q0 = pl.multiple_of(rb * NA_QT, NA_QT)
        k0 = pl.multiple_of(ks * GRID_W, GRID_W)
        q = q_ref[0, pl.ds(q0, NA_QT), :]
        k = k_ref[0, pl.ds(k0, NA_KT), :]
        v = v_ref[0, pl.ds(k0, NA_KT), :]
        pv = []
        for hh, sel in enumerate((low, high)):
            qm = jnp.where(sel, q, jnp.zeros_like(q))
            s = lax.dot_general(qm, k, (((1,), (1,)), ((), ())),
                                preferred_element_type=jnp.float32)
            s = s + bias_ref[0, variant, hh]
            m = jnp.max(s, axis=-1, keepdims=True)
            e = jnp.exp2((s - m).astype(jnp.bfloat16))
            vh = jnp.where(sel, v, jnp.ones_like(v))
            pv.append(jnp.dot(e, vh, preferred_element_type=jnp.float32))
        val, den = _normalise_pair(pv[0], pv[1], low)
        z = z_ref[0, pl.ds(q0, NA_QT), :].astype(jnp.float32)
        o_ref[0, pl.ds(q0, NA_QT), :] = (val / den * (z * jax.nn.sigmoid(z))).astype(jnp.bfloat16)
        return carry

    lax.fori_loop(0, nblk, body, 0, unroll=4)


def _na_attn(proj, bias):
    b, s, _ = proj.shape
    rows = s // GRID_W
    npair = NA_HEADS // 2

    def col(off):
        return lambda i, p: (i, 0, off // LANES + p)

    return pl.pallas_call(
        functools.partial(_na_kernel, rows=rows),
        grid=(b, npair),
        in_specs=[pl.BlockSpec((1, s, LANES), col(OFF_QA)),
                  pl.BlockSpec((1, s, LANES), col(OFF_KA)),
                  pl.BlockSpec((1, s, LANES), col(OFF_VA)),
                  pl.BlockSpec((1, s, LANES), col(OFF_ZA)),
                  pl.BlockSpec((1, 3, 2, NA_QT, NA_KT), lambda i, p: (p, 0, 0, 0, 0))],
        out_specs=pl.BlockSpec((1, s, LANES), lambda i, p: (i, 0, p)),
        out_shape=jax.ShapeDtypeStruct((b, s, NA_WIDTH), jnp.bfloat16),
        compiler_params=pltpu.CompilerParams(
            dimension_semantics=("parallel", "parallel"), vmem_limit_bytes=VMEM_LIMIT),
        name="na_attn",
    )(proj, proj, proj, proj, bias)


def _sw_bias_tables():
    slopes = (2.0 ** (-8.0 * (np.arange(SW_HEADS) + 1) / SW_HEADS)).astype(np.float32)
    i = np.arange(SW_BLOCK)
    col = np.arange(SW_KT)
    out = np.empty((3, SW_KV_HEADS, SW_GROUP, SW_BLOCK, SW_KT), np.float32)
    for var in range(3):
        dist = np.abs(col[None, :] - var * SW_BLOCK - i[:, None])
        for g in range(SW_KV_HEADS):
            for jj in range(SW_GROUP):
                bias = -(slopes[g * SW_GROUP + jj] * dist.astype(np.float32)) * np.float32(LOG2E)
                out[var, g, jj] = np.where(dist <= SW_WINDOW, bias, NEG)
    return out.reshape(3, SW_KV_HEADS, SW_GROUP * SW_BLOCK, SW_KT)


def _sw_kernel(sink_ref, q_ref, k_ref, v_ref, z_ref, bias_ref, o_ref, *, seq):
    nblk_total = seq // SW_BLOCK
    nblk = SW_CHUNK // SW_BLOCK
    chunk = pl.program_id(1)
    low = _low_lanes()
    high = jnp.logical_not(low)

    def body(i, carry):
        nb = chunk * nblk + i
        start = jnp.clip((nb - 1) * SW_BLOCK, 0, seq - SW_KT)
        variant = jnp.where(nb == 0, 0, jnp.where(nb == nblk_total - 1, 2, 1))
        q0 = pl.multiple_of(i * SW_BLOCK, SW_BLOCK)
        k0 = pl.multiple_of(start, SW_BLOCK)
        qb = q_ref[0, pl.ds(q0, SW_BLOCK), :]
        lhs = jnp.concatenate([qb[:, jj * LANES:(jj + 1) * LANES] for jj in range(SW_GROUP)], axis=0)
        k = k_ref[0, pl.ds(k0, SW_KT), :]
        v = v_ref[0, pl.ds(k0, SW_KT), :]
        pv, sink_e = [], []
        for g, sel in enumerate((low, high)):
            kg = jnp.where(sel, k, jnp.zeros_like(k))
            s = lax.dot_general(lhs, kg, (((1,), (1,)), ((), ())),
                                preferred_element_type=jnp.float32)
            s = s + bias_ref[variant, g]
            es, ss = [], []
            for jj in range(SW_GROUP):
                sj = s[jj * SW_BLOCK:(jj + 1) * SW_BLOCK]
                sink = sink_ref[g * SW_GROUP + jj]
                mj = jnp.maximum(jnp.max(sj, axis=-1, keepdims=True), sink)
                es.append(jnp.exp2((sj - mj).astype(jnp.bfloat16)))
                ss.append(jnp.exp2(sink - mj))
            e = jnp.concatenate(es, axis=0)
            sink_e.append(jnp.concatenate(ss, axis=0))
            vg = jnp.where(sel, v, jnp.ones_like(v))
            pv.append(jnp.dot(e, vg, preferred_element_type=jnp.float32))
        val, den = _normalise_pair(pv[0], pv[1], low)
        den = den + jnp.where(low, sink_e[0], sink_e[1])
        o = val / den
        o = jnp.concatenate([o[jj * SW_BLOCK:(jj + 1) * SW_BLOCK] for jj in range(SW_GROUP)], axis=1)
        z = z_ref[0, pl.ds(q0, SW_BLOCK), :].astype(jnp.float32)
        o_ref[0, pl.ds(q0, SW_BLOCK), :] = (o * (z * jax.nn.sigmoid(z))).astype(jnp.bfloat16)
        return carry

    lax.fori_loop(0, nblk, body, 0, unroll=2)


def _sw_attn(proj, bias, sink):
    b, s, _ = proj.shape
    return pl.pallas_call(
        functools.partial(_sw_kernel, seq=s),
        grid=(b, s // SW_CHUNK),
        in_specs=[pl.BlockSpec(memory_space=pltpu.SMEM),
                  pl.BlockSpec((1, SW_CHUNK, SW_WIDTH), lambda i, c: (i, c, OFF_QB // SW_WIDTH)),
                  pl.BlockSpec((1, s, LANES), lambda i, c: (i, 0, OFF_KB // LANES)),
                  pl.BlockSpec((1, s, LANES), lambda i, c: (i, 0, OFF_VB // LANES)),
                  pl.BlockSpec((1, SW_CHUNK, SW_WIDTH), lambda i, c: (i, c, OFF_ZB // SW_WIDTH)),
                  pl.BlockSpec(bias.shape, lambda i, c: (0, 0, 0, 0))],
        out_specs=pl.BlockSpec((1, SW_CHUNK, SW_WIDTH), lambda i, c: (i, c, 0)),
        out_shape=jax.ShapeDtypeStruct((b, s, SW_WIDTH), jnp.bfloat16),
        compiler_params=pltpu.CompilerParams(
            dimension_semantics=("parallel", "parallel"), vmem_limit_bytes=VMEM_LIMIT),
        name="sw_attn",
    )(sink, proj, proj, proj, proj, bias)


def _out_kernel(x_ref, oa_ref, ob_ref, ga_ref, gb_ref, gate_ref, g_ref, wpa_ref, wpb_ref, wout_ref, o_ref):
    a = jnp.dot(oa_ref[0], wpa_ref[...], preferred_element_type=jnp.float32)
    bb = jnp.dot(ob_ref[0], wpb_ref[...], preferred_element_type=jnp.float32)
    merged = (jax.nn.sigmoid(ga_ref[0].astype(jnp.float32)) * a
              + jax.nn.sigmoid(gb_ref[0].astype(jnp.float32)) * bb)
    y = jnp.dot(merged.astype(jnp.bfloat16), wout_ref[...], preferred_element_type=jnp.float32)
    ms = jnp.mean(y * y, axis=-1, keepdims=True)
    yn = y * lax.rsqrt(ms + RMS_EPS) * g_ref[...]
    o_ref[0] = x_ref[0] + gate_ref[0] * yn


def _out_proj(x, oa, ob, proj, gate, g_post, w_pa, w_pb, w_out):
    b, s, d = x.shape
    tok = lambda i, j: (i, j, 0)
    const = lambda i, j: (0, 0)
    return pl.pallas_call(
        _out_kernel,
        grid=(b, s // TM_OUT),
        in_specs=[pl.BlockSpec((1, TM_OUT, d), tok),
                  pl.BlockSpec((1, TM_OUT, NA_WIDTH), tok),
                  pl.BlockSpec((1, TM_OUT, SW_WIDTH), tok),
                  pl.BlockSpec((1, TM_OUT, d), lambda i, j: (i, j, OFF_GA // D_MODEL)),
                  pl.BlockSpec((1, TM_OUT, d), lambda i, j: (i, j, OFF_GB // D_MODEL)),
                  pl.BlockSpec((1, 1, d), lambda i, j: (i, 0, 0)),
                  pl.BlockSpec((1, d), const),
                  pl.BlockSpec((NA_WIDTH, d), const),
                  pl.BlockSpec((SW_WIDTH, d), const),
                  pl.BlockSpec((d, d), const)],
        out_specs=pl.BlockSpec((1, TM_OUT, d), tok),
        out_shape=jax.ShapeDtypeStruct((b, s, d), jnp.float32),
        compiler_params=pltpu.CompilerParams(
            dimension_semantics=("parallel", "parallel"), vmem_limit_bytes=VMEM_LIMIT),
        name="out_proj",
    )(x, oa, ob, proj, proj, gate, g_post, w_pa, w_pb, w_out)


def _encoder_layer(xs, adas, g_pre, g_post, w_in, na_bias, sw_bias, sw_sink, w_pa, w_pb, w_out):
    outs = []
    for x, ada in zip(xs, adas):
        shift, scale, gate = jnp.split(ada[:, None, :], 3, axis=-1)
        proj = _in_proj(x, scale, shift, g_pre, w_in)
        oa = _na_attn(proj, na_bias)
        ob = _sw_attn(proj, sw_bias, sw_sink)
        outs.append(_out_proj(x, oa, ob, proj, gate, g_post, w_pa, w_pb, w_out))
    return outs


def kernel(x_prompt, x_sample, c_prompt, c_sample, w_ada, b_ada, g_pre, g_post, w_in, na_rpb, sw_sink,
           w_pa, w_pb, w_out):
    depth = w_in.shape[0]
    seq = x_prompt.shape[1]
    assert seq % (GRID_W * NA_QROWS) == 0 and seq % SW_CHUNK == 0
    assert x_sample.shape[1] == seq and seq // GRID_W >= 2 * NA_KROWS
    xs = [x_prompt, x_sample]
    nbs = [x_prompt.shape[0], x_sample.shape[0]]
    c_all = jnp.concatenate([c_prompt, c_sample], axis=0)
    sw_bias = jnp.asarray(_sw_bias_tables())
    bf16 = jnp.bfloat16
    for l in range(depth):
        ada = _ada(c_all, w_ada[l], b_ada[l])
        adas = [ada[:nbs[0]], ada[nbs[0]:]]
        w_in_l = (w_in[l][:, IN_COL_PERM] * IN_COL_SCALE).astype(bf16)
        w_pb_l = w_pb[l][SW_HEAD_PERM].astype(bf16)
        na_bias = _na_bias_tables(na_rpb[l], seq // GRID_W)
        sink = sw_sink[l].astype(jnp.float32) * LOG2E
        xs = _encoder_layer(xs, adas, g_pre[l].reshape(1, -1), g_post[l].reshape(1, -1), w_in_l, na_bias,
                            sw_bias, sink, w_pa[l].astype(bf16), w_pb_l, w_out[l].astype(bf16))
    return (xs[0], xs[1])
```

```python
import functools
import math

import jax
import jax.numpy as jnp
import numpy as np
from jax import lax
from jax.experimental import pallas as pl
from jax.experimental.pallas import tpu as pltpu

D_MODEL = 1024
GRID_W = 64
HEAD_DIM = 64
NA_HEADS = 8
NA_WIDTH = NA_HEADS * HEAD_DIM
NA_KH = 8
NA_KH_MAX = 8
NA_KW = 16
SW_HEADS = 8
SW_KV_HEADS = 2
SW_GROUP = SW_HEADS // SW_KV_HEADS
SW_WIDTH = SW_HEADS * HEAD_DIM
SW_KV_WIDTH = SW_KV_HEADS * HEAD_DIM
SW_WINDOW = 128
SW_BLOCK = 128
RMS_EPS = 1e-6
D_IN = 4 * NA_WIDTH + 2 * SW_WIDTH + 2 * SW_KV_WIDTH + 2 * D_MODEL

LANES = 128
NEG = -1e30
LOG2E = math.log2(math.e)
VMEM_LIMIT = 56 * 1024 * 1024

OFF_GA, OFF_GB = 0, D_MODEL
OFF_QA = 2 * D_MODEL
OFF_KA = OFF_QA + NA_WIDTH
OFF_VA = OFF_KA + NA_WIDTH
OFF_ZA = OFF_VA + NA_WIDTH
OFF_QB = OFF_ZA + NA_WIDTH
OFF_ZB = OFF_QB + SW_WIDTH
OFF_KB = OFF_ZB + SW_WIDTH
OFF_VB = OFF_KB + SW_KV_WIDTH
SW_HEAD_ORDER = (0, 4, 1, 5, 2, 6, 3, 7)

NA_QROWS = 4
NA_KROWS = 12
NA_QT = NA_QROWS * GRID_W
NA_KT = NA_KROWS * GRID_W
SW_KT = 3 * SW_BLOCK

TM_IN = 512
CN_IN = 768
TM_OUT = 512
SW_CHUNK = 2048


def _in_col_perm():
    o = np.cumsum((0, NA_WIDTH, NA_WIDTH, NA_WIDTH, NA_WIDTH, SW_WIDTH, SW_KV_WIDTH, SW_KV_WIDTH, SW_WIDTH,
                   D_MODEL))
    qa, ka, va, za, qb, kb, vb, zb, ga, gb = (int(v) for v in o)
    ar = np.arange
    heads = np.concatenate([h * HEAD_DIM + ar(HEAD_DIM) for h in SW_HEAD_ORDER])
    perm = np.concatenate([
        ga + ar(D_MODEL), gb + ar(D_MODEL),
        qa + ar(NA_WIDTH), ka + ar(NA_WIDTH), va + ar(NA_WIDTH), za + ar(NA_WIDTH),
        qb + heads, zb + heads, kb + ar(SW_KV_WIDTH), vb + ar(SW_KV_WIDTH)])
    col_scale = np.ones((D_IN,), np.float32)
    col_scale[OFF_QA:OFF_QA + NA_WIDTH] = HEAD_DIM ** -0.5 * LOG2E
    col_scale[OFF_QB:OFF_QB + SW_WIDTH] = HEAD_DIM ** -0.5 * LOG2E
    return perm, col_scale, heads


IN_COL_PERM, IN_COL_SCALE, SW_HEAD_PERM = _in_col_perm()


def _low_lanes():
    return lax.broadcasted_iota(jnp.int32, (1, LANES), 1) < HEAD_DIM


def _ada_kernel(c_ref, w_ref, b_ref, o_ref):
    c = c_ref[...]
    sc = c * jax.nn.sigmoid(c)
    o_ref[...] = jnp.dot(sc, w_ref[...], preferred_element_type=jnp.float32,
                         precision=lax.Precision.HIGHEST) + b_ref[...]


def _ada(c, w_ada, b_ada):
    nb, d = c.shape
    n = w_ada.shape[1]
    tn = 1024
    return pl.pallas_call(
        _ada_kernel,
        grid=(n // tn,),
        in_specs=[pl.BlockSpec((nb, d), lambda j: (0, 0)),
                  pl.BlockSpec((d, tn), lambda j: (0, j)),
                  pl.BlockSpec((1, tn), lambda j: (0, j))],
        out_specs=pl.BlockSpec((nb, tn), lambda j: (0, j)),
        out_shape=jax.ShapeDtypeStruct((nb, n), jnp.float32),
        name="ada",
    )(c, w_ada, b_ada.reshape(1, n))


def _in_proj_kernel(x_ref, scale_ref, shift_ref, g_ref, w_ref, o_ref):
    x = x_ref[0]
    ms = jnp.mean(x * x, axis=-1, keepdims=True)
    y = x * lax.rsqrt(ms + RMS_EPS) * g_ref[...]
    h = (y * (1.0 + scale_ref[0]) + shift_ref[0]).astype(jnp.bfloat16)
    for c in range(D_IN // CN_IN):
        cols = slice(c * CN_IN, (c + 1) * CN_IN)
        o_ref[0, :, cols] = jnp.dot(h, w_ref[:, cols],
                                    preferred_element_type=jnp.float32).astype(jnp.bfloat16)


def _in_proj(x, scale, shift, g_pre, w_in):
    b, s, d = x.shape
    return pl.pallas_call(
        _in_proj_kernel,
        grid=(b, s // TM_IN),
        in_specs=[pl.BlockSpec((1, TM_IN, d), lambda i, j: (i, j, 0)),
                  pl.BlockSpec((1, 1, d), lambda i, j: (i, 0, 0)),
                  pl.BlockSpec((1, 1, d), lambda i, j: (i, 0, 0)),
                  pl.BlockSpec((1, d), lambda i, j: (0, 0)),
                  pl.BlockSpec((d, D_IN), lambda i, j: (0, 0))],
        out_specs=pl.BlockSpec((1, TM_IN, D_IN), lambda i, j: (i, j, 0)),
        out_shape=jax.ShapeDtypeStruct((b, s, D_IN), jnp.bfloat16),
        compiler_params=pltpu.CompilerParams(
            dimension_semantics=("parallel", "parallel"), vmem_limit_bytes=VMEM_LIMIT),
        name="in_proj",
    )(x, scale, shift, g_pre, w_in)


def _normalise_pair(pv_low, pv_high, low):
    val = jnp.where(low, pv_low, pv_high)
    den = pltpu.roll(jnp.where(low, pv_high, pv_low), HEAD_DIM, axis=1)
    return val, den


def _na_bias_tables(rpb, rows):
    pad = NA_QROWS
    c = np.arange(GRID_W)
    c0 = np.clip(c - NA_KW // 2, 0, GRID_W - NA_KW)
    col_ok = (c[None, :] >= c0[:, None]) & (c[None, :] < c0[:, None] + NA_KW)
    cpad = GRID_W - NA_KW
    rpb = jnp.pad(rpb.astype(jnp.float32) * LOG2E, ((0, 0), (0, 0), (cpad, cpad)))
    t = jnp.stack([rpb[:, :, GRID_W - 1 - ci:2 * GRID_W - 1 - ci] for ci in range(GRID_W)], axis=1)
    t = jnp.where(col_ok[None, :, None, :], t, NEG)
    t = jnp.pad(t, ((0, 0), (0, 0), (pad, pad), (0, 0)), constant_values=NEG)
    j = np.arange(NA_KROWS)
    blocks = []
    for r, ks in ((0, 0), (NA_QROWS, 0), (rows - NA_QROWS, rows - NA_KROWS)):
        strips = []
        for a in range(NA_QROWS):
            rq = r + a
            r0 = min(max(rq - NA_KH // 2, 0), rows - NA_KH)
            row_ok = (ks + j >= r0) & (ks + j < r0 + NA_KH)
            d0 = ks - rq + NA_KH_MAX - 1 + pad
            assert 0 <= d0 and d0 + NA_KROWS <= t.shape[2]
            strip = jnp.where(row_ok[None, None, :, None], t[:, :, d0:d0 + NA_KROWS, :], NEG)
            strips.append(strip.reshape(NA_HEADS, GRID_W, NA_KT))
        blocks.append(jnp.concatenate(strips, axis=1))
    t = jnp.stack(blocks, axis=1)
    return t.reshape(NA_HEADS // 2, 2, 3, NA_QT, NA_KT).transpose(0, 2, 1, 3, 4)


def _na_kernel(q_ref, k_ref, v_ref, z_ref, bias_ref, o_ref, x0_ref, x1_ref, *, rows):
    nblk = rows // NA_QROWS
    low = _low_lanes()
    high = jnp.logical_not(low)

    def key_start(rb):
        ks = jnp.clip(rb * NA_QROWS - NA_KH // 2, 0, rows - NA_KROWS)
        return pl.multiple_of(ks * GRID_W, GRID_W)

    def scores(rb, x_ref):
        variant = jnp.where(rb == 0, 0, jnp.where(rb == nblk - 1, 2, 1))
        q = q_ref[0, pl.ds(pl.multiple_of(rb * NA_QT, NA_QT), NA_QT), :]
        k = k_ref[0, pl.ds(key_start(rb), NA_KT), :]
        for hh, sel in enumerate((low, high)):
            qm = jnp.where(sel, q, jnp.zeros_like(q))
            s = lax.dot_general(qm, k, (((1,), (1,)), ((), ())),
                                preferred_element_type=jnp.float32)
            s = s + bias_ref[0, variant, hh]
            m = jnp.max(s, axis=-1, keepdims=True)
            x_ref[hh] = (s - m).astype(jnp.bfloat16)

    def values(rb, x_ref):
        q0 = pl.multiple_of(rb * NA_QT, NA_QT)
        v = v_ref[0, pl.ds(key_start(rb), NA_KT), :]
        pv = []
        for hh, sel in enumerate((low, high)):
            vh = jnp.where(sel, v, jnp.ones_like(v))
            pv.append(jnp.dot(jnp.exp2(x_ref[hh]), vh, preferred_element_type=jnp.float32))
        val, den = _normalise_pair(pv[0], pv[1], low)
        z = z_ref[0, pl.ds(q0, NA_QT), :].astype(jnp.float32)
        o_ref[0, pl.ds(q0, NA_QT), :] = (val / den * (z * jax.nn.sigmoid(z))).astype(jnp.bfloat16)

    scores(0, x0_ref)

    def body(i, carry):
        rb = 2 * i
        scores(rb + 1, x1_ref)
        values(rb, x0_ref)
        scores(jnp.minimum(rb + 2, nblk - 1), x0_ref)
        values(rb + 1, x1_ref)
        return carry

    lax.fori_loop(0, nblk // 2, body, 0, unroll=2)


def _na_attn(proj, bias):
    b, s, _ = proj.shape
    rows = s // GRID_W
    npair = NA_HEADS // 2

    def col(off):
        return lambda i, p: (i, 0, off // LANES + p)

    return pl.pallas_call(
        functools.partial(_na_kernel, rows=rows),
        grid=(b, npair),
        in_specs=[pl.BlockSpec((1, s, LANES), col(OFF_QA)),
                  pl.BlockSpec((1, s, LANES), col(OFF_KA)),
                  pl.BlockSpec((1, s, LANES), col(OFF_VA)),
                  pl.BlockSpec((1, s, LANES), col(OFF_ZA)),
                  pl.BlockSpec((1, 3, 2, NA_QT, NA_KT), lambda i, p: (p, 0, 0, 0, 0))],
        out_specs=pl.BlockSpec((1, s, LANES), lambda i, p: (i, 0, p)),
        out_shape=jax.ShapeDtypeStruct((b, s, NA_WIDTH), jnp.bfloat16),
        scratch_shapes=[pltpu.VMEM((2, NA_QT, NA_KT), jnp.bfloat16),
                        pltpu.VMEM((2, NA_QT, NA_KT), jnp.bfloat16)],
        compiler_params=pltpu.CompilerParams(
            dimension_semantics=("parallel", "parallel"), vmem_limit_bytes=VMEM_LIMIT),
        name="na_attn",
    )(proj, proj, proj, proj, bias)


def _sw_bias_tables():
    slopes = (2.0 ** (-8.0 * (np.arange(SW_HEADS) + 1) / SW_HEADS)).astype(np.float32)
    i = np.arange(SW_BLOCK)
    col = np.arange(SW_KT)
    out = np.empty((3, SW_KV_HEADS, SW_GROUP, SW_BLOCK, SW_KT), np.float32)
    for var in range(3):
        dist = np.abs(col[None, :] - var * SW_BLOCK - i[:, None])
        for g in range(SW_KV_HEADS):
            for jj in range(SW_GROUP):
                bias = -(slopes[g * SW_GROUP + jj] * dist.astype(np.float32)) * np.float32(LOG2E)
                out[var, g, jj] = np.where(dist <= SW_WINDOW, bias, NEG)
    return out.reshape(3, SW_KV_HEADS, SW_GROUP * SW_BLOCK, SW_KT)


def _sw_kernel(sink_ref, q_ref, k_ref, v_ref, z_ref, bias_ref, o_ref, *, seq):
    nblk_total = seq // SW_BLOCK
    nblk = SW_CHUNK // SW_BLOCK
    chunk = pl.program_id(1)
    low = _low_lanes()
    high = jnp.logical_not(low)

    def body(i, carry):
        nb = chunk * nblk + i
        start = jnp.clip((nb - 1) * SW_BLOCK, 0, seq - SW_KT)
        variant = jnp.where(nb == 0, 0, jnp.where(nb == nblk_total - 1, 2, 1))
        q0 = pl.multiple_of(i * SW_BLOCK, SW_BLOCK)
        k0 = pl.multiple_of(start, SW_BLOCK)
        qb = q_ref[0, pl.ds(q0, SW_BLOCK), :]
        lhs = jnp.concatenate([qb[:, jj * LANES:(jj + 1) * LANES] for jj in range(SW_GROUP)], axis=0)
        k = k_ref[0, pl.ds(k0, SW_KT), :]
        v = v_ref[0, pl.ds(k0, SW_KT), :]
        pv, sink_e = [], []
        for g, sel in enumerate((low, high)):
            kg = jnp.where(sel, k, jnp.zeros_like(k))
            s = lax.dot_general(lhs, kg, (((1,), (1,)), ((), ())),
                                preferred_element_type=jnp.float32)
            s = s + bias_ref[variant, g]
            es, ss = [], []
            for jj in range(SW_GROUP):
                sj = s[jj * SW_BLOCK:(jj + 1) * SW_BLOCK]
                sink = sink_ref[g * SW_GROUP + jj]
                mj = jnp.maximum(jnp.max(sj, axis=-1, keepdims=True), sink)
                es.append(jnp.exp2((sj - mj).astype(jnp.bfloat16)))
                ss.append(jnp.exp2(sink - mj))
            e = jnp.concatenate(es, axis=0)
            sink_e.append(jnp.concatenate(ss, axis=0))
            vg = jnp.where(sel, v, jnp.ones_like(v))
            pv.append(jnp.dot(e, vg, preferred_element_type=jnp.float32))
        val, den = _normalise_pair(pv[0], pv[1], low)
        den = den + jnp.where(low, sink_e[0], sink_e[1])
        o = val / den
        o = jnp.concatenate([o[jj * SW_BLOCK:(jj + 1) * SW_BLOCK] for jj in range(SW_GROUP)], axis=1)
        z = z_ref[0, pl.ds(q0, SW_BLOCK), :].astype(jnp.float32)
        o_ref[0, pl.ds(q0, SW_BLOCK), :] = (o * (z * jax.nn.sigmoid(z))).astype(jnp.bfloat16)
        return carry

    lax.fori_loop(0, nblk, body, 0, unroll=2)


def _sw_attn(proj, bias, sink):
    b, s, _ = proj.shape
    return pl.pallas_call(
        functools.partial(_sw_kernel, seq=s),
        grid=(b, s // SW_CHUNK),
        in_specs=[pl.BlockSpec(memory_space=pltpu.SMEM),
                  pl.BlockSpec((1, SW_CHUNK, SW_WIDTH), lambda i, c: (i, c, OFF_QB // SW_WIDTH)),
                  pl.BlockSpec((1, s, LANES), lambda i, c: (i, 0, OFF_KB // LANES)),
                  pl.BlockSpec((1, s, LANES), lambda i, c: (i, 0, OFF_VB // LANES)),
                  pl.BlockSpec((1, SW_CHUNK, SW_WIDTH), lambda i, c: (i, c, OFF_ZB // SW_WIDTH)),
                  pl.BlockSpec(bias.shape, lambda i, c: (0, 0, 0, 0))],
        out_specs=pl.BlockSpec((1, SW_CHUNK, SW_WIDTH), lambda i, c: (i, c, 0)),
        out_shape=jax.ShapeDtypeStruct((b, s, SW_WIDTH), jnp.bfloat16),
        compiler_params=pltpu.CompilerParams(
            dimension_semantics=("parallel", "parallel"), vmem_limit_bytes=VMEM_LIMIT),
        name="sw_attn",
    )(sink, proj, proj, proj, proj, bias)


def _out_kernel(x_ref, oa_ref, ob_ref, ga_ref, gb_ref, gate_ref, g_ref, wpa_ref, wpb_ref, wout_ref, o_ref):
    a = jnp.dot(oa_ref[0], wpa_ref[...], preferred_element_type=jnp.float32)
    bb = jnp.dot(ob_ref[0], wpb_ref[...], preferred_element_type=jnp.float32)
    merged = (jax.nn.sigmoid(ga_ref[0].astype(jnp.float32)) * a
              + jax.nn.sigmoid(gb_ref[0].astype(jnp.float32)) * bb)
    y = jnp.dot(merged.astype(jnp.bfloat16), wout_ref[...], preferred_element_type=jnp.float32)
    ms = jnp.mean(y * y, axis=-1, keepdims=True)
    yn = y * lax.rsqrt(ms + RMS_EPS) * g_ref[...]
    o_ref[0] = x_ref[0] + gate_ref[0] * yn


def _out_proj(x, oa, ob, proj, gate, g_post, w_pa, w_pb, w_out):
    b, s, d = x.shape
    tok = lambda i, j: (i, j, 0)
    const = lambda i, j: (0, 0)
    return pl.pallas_call(
        _out_kernel,
        grid=(b, s // TM_OUT),
        in_specs=[pl.BlockSpec((1, TM_OUT, d), tok),
                  pl.BlockSpec((1, TM_OUT, NA_WIDTH), tok),
                  pl.BlockSpec((1, TM_OUT, SW_WIDTH), tok),
                  pl.BlockSpec((1, TM_OUT, d), lambda i, j: (i, j, OFF_GA // D_MODEL)),
                  pl.BlockSpec((1, TM_OUT, d), lambda i, j: (i, j, OFF_GB // D_MODEL)),
                  pl.BlockSpec((1, 1, d), lambda i, j: (i, 0, 0)),
                  pl.BlockSpec((1, d), const),
                  pl.BlockSpec((NA_WIDTH, d), const),
                  pl.BlockSpec((SW_WIDTH, d), const),
                  pl.BlockSpec((d, d), const)],
        out_specs=pl.BlockSpec((1, TM_OUT, d), tok),
        out_shape=jax.ShapeDtypeStruct((b, s, d), jnp.float32),
        compiler_params=pltpu.CompilerParams(
            dimension_semantics=("parallel", "parallel"), vmem_limit_bytes=VMEM_LIMIT),
        name="out_proj",
    )(x, oa, ob, proj, proj, gate, g_post, w_pa, w_pb, w_out)


def _encoder_layer(xs, adas, g_pre, g_post, w_in, na_bias, sw_bias, sw_sink, w_pa, w_pb, w_out):
    outs = []
    for x, ada in zip(xs, adas):
        shift, scale, gate = jnp.split(ada[:, None, :], 3, axis=-1)
        proj = _in_proj(x, scale, shift, g_pre, w_in)
        oa = _na_attn(proj, na_bias)
        ob = _sw_attn(proj, sw_bias, sw_sink)
        outs.append(_out_proj(x, oa, ob, proj, gate, g_post, w_pa, w_pb, w_out))
    return outs


def kernel(x_prompt, x_sample, c_prompt, c_sample, w_ada, b_ada, g_pre, g_post, w_in, na_rpb, sw_sink,
           w_pa, w_pb, w_out):
    depth = w_in.shape[0]
    seq = x_prompt.shape[1]
    assert seq % (GRID_W * NA_QROWS) == 0 and seq % SW_CHUNK == 0
    assert x_sample.shape[1] == seq and seq // GRID_W >= 2 * NA_KROWS
    xs = [x_prompt, x_sample]
    nbs = [x_prompt.shape[0], x_sample.shape[0]]
    c_all = jnp.concatenate([c_prompt, c_sample], axis=0)
    sw_bias = jnp.asarray(_sw_bias_tables())
    bf16 = jnp.bfloat16
    for l in range(depth):
        ada = _ada(c_all, w_ada[l], b_ada[l])
        adas = [ada[:nbs[0]], ada[nbs[0]:]]
        w_in_l = (w_in[l][:, IN_COL_PERM] * IN_COL_SCALE).astype(bf16)
        w_pb_l = w_pb[l][SW_HEAD_PERM].astype(bf16)
        na_bias = _na_bias_tables(na_rpb[l], seq // GRID_W)
        sink = sw_sink[l].astype(jnp.float32) * LOG2E
        xs = _encoder_layer(xs, adas, g_pre[l].reshape(1, -1), g_post[l].reshape(1, -1), w_in_l, na_bias,
                            sw_bias, sink, w_pa[l].astype(bf16), w_pb_l, w_out[l].astype(bf16))
    return (xs[0], xs[1])
```

```python
import functools
import math

import jax
import jax.numpy as jnp
import numpy as np
from jax import lax
from jax.experimental import pallas as pl
from jax.experimental.pallas import tpu as pltpu

D_MODEL = 1024
GRID_W = 64
HEAD_DIM = 64
NA_HEADS = 8
NA_WIDTH = NA_HEADS * HEAD_DIM
NA_KH = 8
NA_KH_MAX = 8
NA_KW = 16
SW_HEADS = 8
SW_KV_HEADS = 2
SW_GROUP = SW_HEADS // SW_KV_HEADS
SW_WIDTH = SW_HEADS * HEAD_DIM
SW_KV_WIDTH = SW_KV_HEADS * HEAD_DIM
SW_WINDOW = 128
SW_BLOCK = 128
RMS_EPS = 1e-6
D_IN = 4 * NA_WIDTH + 2 * SW_WIDTH + 2 * SW_KV_WIDTH + 2 * D_MODEL

LANES = 128
NEG = -1e30
LOG2E = math.log2(math.e)
VMEM_LIMIT = 56 * 1024 * 1024

OFF_GA, OFF_GB = 0, D_MODEL
OFF_QA = 2 * D_MODEL
OFF_KA = OFF_QA + NA_WIDTH
OFF_VA = OFF_KA + NA_WIDTH
OFF_ZA = OFF_VA + NA_WIDTH
OFF_QB = OFF_ZA + NA_WIDTH
OFF_ZB = OFF_QB + SW_WIDTH
OFF_KB = OFF_ZB + SW_WIDTH
OFF_VB = OFF_KB + SW_KV_WIDTH
SW_HEAD_ORDER = (0, 4, 1, 5, 2, 6, 3, 7)

NA_QROWS = 4
NA_KROWS = 12
NA_QT = NA_QROWS * GRID_W
NA_KT = NA_KROWS * GRID_W
SW_KT = 3 * SW_BLOCK

TM_IN = 512
CN_IN = 768
TM_OUT = 1024
SW_CHUNK = 2048


def _in_col_perm():
    o = np.cumsum((0, NA_WIDTH, NA_WIDTH, NA_WIDTH, NA_WIDTH, SW_WIDTH, SW_KV_WIDTH, SW_KV_WIDTH, SW_WIDTH,
                   D_MODEL))
    qa, ka, va, za, qb, kb, vb, zb, ga, gb = (int(v) for v in o)
    ar = np.arange
    heads = np.concatenate([h * HEAD_DIM + ar(HEAD_DIM) for h in SW_HEAD_ORDER])
    perm = np.concatenate([
        ga + ar(D_MODEL), gb + ar(D_MODEL),
        qa + ar(NA_WIDTH), ka + ar(NA_WIDTH), va + ar(NA_WIDTH), za + ar(NA_WIDTH),
        qb + heads, zb + heads, kb + ar(SW_KV_WIDTH), vb + ar(SW_KV_WIDTH)])
    col_scale = np.ones((D_IN,), np.float32)
    col_scale[OFF_QA:OFF_QA + NA_WIDTH] = HEAD_DIM ** -0.5 * LOG2E
    col_scale[OFF_QB:OFF_QB + SW_WIDTH] = HEAD_DIM ** -0.5 * LOG2E
    return perm, col_scale, heads


IN_COL_PERM, IN_COL_SCALE, SW_HEAD_PERM = _in_col_perm()


def _low_lanes():
    return lax.broadcasted_iota(jnp.int32, (1, LANES), 1) < HEAD_DIM


def _ada_kernel(c_ref, w_ref, b_ref, o_ref):
    c = c_ref[...]
    sc = c * jax.nn.sigmoid(c)
    o_ref[...] = jnp.dot(sc, w_ref[...], preferred_element_type=jnp.float32,
                         precision=lax.Precision.HIGHEST) + b_ref[...]


def _ada(c, w_ada, b_ada):
    nb, d = c.shape
    n = w_ada.shape[1]
    tn = 1024
    return pl.pallas_call(
        _ada_kernel,
        grid=(n // tn,),
        in_specs=[pl.BlockSpec((nb, d), lambda j: (0, 0)),
                  pl.BlockSpec((d, tn), lambda j: (0, j)),
                  pl.BlockSpec((1, tn), lambda j: (0, j))],
        out_specs=pl.BlockSpec((nb, tn), lambda j: (0, j)),
        out_shape=jax.ShapeDtypeStruct((nb, n), jnp.float32),
        name="ada",
    )(c, w_ada, b_ada.reshape(1, n))


def _in_proj_kernel(x_ref, scale_ref, shift_ref, g_ref, w_ref, o_ref):
    x = x_ref[0]
    ms = jnp.mean(x * x, axis=-1, keepdims=True)
    y = x * lax.rsqrt(ms + RMS_EPS) * g_ref[...]
    h = (y * (1.0 + scale_ref[0]) + shift_ref[0]).astype(jnp.bfloat16)
    for c in range(D_IN // CN_IN):
        cols = slice(c * CN_IN, (c + 1) * CN_IN)
        o_ref[0, :, cols] = jnp.dot(h, w_ref[:, cols],
                                    preferred_element_type=jnp.float32).astype(jnp.bfloat16)


def _in_proj(x, scale, shift, g_pre, w_in):
    b, s, d = x.shape
    return pl.pallas_call(
        _in_proj_kernel,
        grid=(b, s // TM_IN),
        in_specs=[pl.BlockSpec((1, TM_IN, d), lambda i, j: (i, j, 0)),
                  pl.BlockSpec((1, 1, d), lambda i, j: (i, 0, 0)),
                  pl.BlockSpec((1, 1, d), lambda i, j: (i, 0, 0)),
                  pl.BlockSpec((1, d), lambda i, j: (0, 0)),
                  pl.BlockSpec((d, D_IN), lambda i, j: (0, 0))],
        out_specs=pl.BlockSpec((1, TM_IN, D_IN), lambda i, j: (i, j, 0)),
        out_shape=jax.ShapeDtypeStruct((b, s, D_IN), jnp.bfloat16),
        compiler_params=pltpu.CompilerParams(
            dimension_semantics=("parallel", "parallel"), vmem_limit_bytes=VMEM_LIMIT),
        name="in_proj",
    )(x, scale, shift, g_pre, w_in)


def _normalise_pair(pv_low, pv_high, low):
    val = jnp.where(low, pv_low, pv_high)
    den = pltpu.roll(jnp.where(low, pv_high, pv_low), HEAD_DIM, axis=1)
    return val, den


def _na_bias_tables(rpb, rows):
    pad = NA_QROWS
    c = np.arange(GRID_W)
    c0 = np.clip(c - NA_KW // 2, 0, GRID_W - NA_KW)
    col_ok = (c[None, :] >= c0[:, None]) & (c[None, :] < c0[:, None] + NA_KW)
    cpad = GRID_W - NA_KW
    rpb = jnp.pad(rpb.astype(jnp.float32) * LOG2E, ((0, 0), (0, 0), (cpad, cpad)))
    t = jnp.stack([rpb[:, :, GRID_W - 1 - ci:2 * GRID_W - 1 - ci] for ci in range(GRID_W)], axis=1)
    t = jnp.where(col_ok[None, :, None, :], t, NEG)
    t = jnp.pad(t, ((0, 0), (0, 0), (pad, pad), (0, 0)), constant_values=NEG)
    j = np.arange(NA_KROWS)
    blocks = []
    for r, ks in ((0, 0), (NA_QROWS, 0), (rows - NA_QROWS, rows - NA_KROWS)):
        strips = []
        for a in range(NA_QROWS):
            rq = r + a
            r0 = min(max(rq - NA_KH // 2, 0), rows - NA_KH)
            row_ok = (ks + j >= r0) & (ks + j < r0 + NA_KH)
            d0 = ks - rq + NA_KH_MAX - 1 + pad
            assert 0 <= d0 and d0 + NA_KROWS <= t.shape[2]
            strip = jnp.where(row_ok[None, None, :, None], t[:, :, d0:d0 + NA_KROWS, :], NEG)
            strips.append(strip.reshape(NA_HEADS, GRID_W, NA_KT))
        blocks.append(jnp.concatenate(strips, axis=1))
    t = jnp.stack(blocks, axis=1)
    return t.reshape(NA_HEADS // 2, 2, 3, NA_QT, NA_KT).transpose(0, 2, 1, 3, 4)


def _na_kernel(q_ref, k_ref, v_ref, z_ref, bias_ref, o_ref, x0_ref, x1_ref, *, rows):
    nblk = rows // NA_QROWS
    low = _low_lanes()
    high = jnp.logical_not(low)

    def key_start(rb):
        ks = jnp.clip(rb * NA_QROWS - NA_KH // 2, 0, rows - NA_KROWS)
        return pl.multiple_of(ks * GRID_W, GRID_W)

    def scores(rb, x_ref):
        variant = jnp.where(rb == 0, 0, jnp.where(rb == nblk - 1, 2, 1))
        q = q_ref[0, pl.ds(pl.multiple_of(rb * NA_QT, NA_QT), NA_QT), :]
        k = k_ref[0, pl.ds(key_start(rb), NA_KT), :]
        for hh, sel in enumerate((low, high)):
            qm = jnp.where(sel, q, jnp.zeros_like(q))
            s = lax.dot_general(qm, k, (((1,), (1,)), ((), ())),
                                preferred_element_type=jnp.float32)
            s = s + bias_ref[0, variant, hh]
            m = jnp.max(s, axis=-1, keepdims=True)
            x_ref[hh] = (s - m).astype(jnp.bfloat16)

    def values(rb, x_ref):
        q0 = pl.multiple_of(rb * NA_QT, NA_QT)
        v = v_ref[0, pl.ds(key_start(rb), NA_KT), :]
        pv = []
        for hh, sel in enumerate((low, high)):
            vh = jnp.where(sel, v, jnp.ones_like(v))
            pv.append(jnp.dot(jnp.exp2(x_ref[hh]), vh, preferred_element_type=jnp.float32))
        val, den = _normalise_pair(pv[0], pv[1], low)
        z = z_ref[0, pl.ds(q0, NA_QT), :].astype(jnp.float32)
        o_ref[0, pl.ds(q0, NA_QT), :] = (val / den * (z * jax.nn.sigmoid(z))).astype(jnp.bfloat16)

    scores(0, x0_ref)

    def body(i, carry):
        rb = 2 * i
        scores(rb + 1, x1_ref)
        values(rb, x0_ref)
        scores(jnp.minimum(rb + 2, nblk - 1), x0_ref)
        values(rb + 1, x1_ref)
        return carry

    lax.fori_loop(0, nblk // 2, body, 0, unroll=4)


def _na_attn(proj, bias):
    b, s, _ = proj.shape
    rows = s // GRID_W
    npair = NA_HEADS // 2

    def col(off):
        return lambda i, p: (i, 0, off // LANES + p)

    return pl.pallas_call(
        functools.partial(_na_kernel, rows=rows),
        grid=(b, npair),
        in_specs=[pl.BlockSpec((1, s, LANES), col(OFF_QA)),
                  pl.BlockSpec((1, s, LANES), col(OFF_KA)),
                  pl.BlockSpec((1, s, LANES), col(OFF_VA)),
                  pl.BlockSpec((1, s, LANES), col(OFF_ZA)),
                  pl.BlockSpec((1, 3, 2, NA_QT, NA_KT), lambda i, p: (p, 0, 0, 0, 0))],
        out_specs=pl.BlockSpec((1, s, LANES), lambda i, p: (i, 0, p)),
        out_shape=jax.ShapeDtypeStruct((b, s, NA_WIDTH), jnp.bfloat16),
        scratch_shapes=[pltpu.VMEM((2, NA_QT, NA_KT), jnp.bfloat16),
                        pltpu.VMEM((2, NA_QT, NA_KT), jnp.bfloat16)],
        compiler_params=pltpu.CompilerParams(
            dimension_semantics=("parallel", "parallel"), vmem_limit_bytes=VMEM_LIMIT),
        name="na_attn",
    )(proj, proj, proj, proj, bias)


def _sw_bias_tables():
    slopes = (2.0 ** (-8.0 * (np.arange(SW_HEADS) + 1) / SW_HEADS)).astype(np.float32)
    i = np.arange(SW_BLOCK)
    col = np.arange(SW_KT)
    out = np.empty((3, SW_KV_HEADS, SW_GROUP, SW_BLOCK, SW_KT), np.float32)
    for var in range(3):
        dist = np.abs(col[None, :] - var * SW_BLOCK - i[:, None])
        for g in range(SW_KV_HEADS):
            for jj in range(SW_GROUP):
                bias = -(slopes[g * SW_GROUP + jj] * dist.astype(np.float32)) * np.float32(LOG2E)
                out[var, g, jj] = np.where(dist <= SW_WINDOW, bias, NEG)
    return out.reshape(3, SW_KV_HEADS, SW_GROUP * SW_BLOCK, SW_KT)


def _sw_kernel(sink_ref, q_ref, k_ref, v_ref, z_ref, bias_ref, o_ref, *, seq):
    nblk_total = seq // SW_BLOCK
    nblk = SW_CHUNK // SW_BLOCK
    chunk = pl.program_id(1)
    low = _low_lanes()
    high = jnp.logical_not(low)

    def body(i, carry):
        nb = chunk * nblk + i
        start = jnp.clip((nb - 1) * SW_BLOCK, 0, seq - SW_KT)
        variant = jnp.where(nb == 0, 0, jnp.where(nb == nblk_total - 1, 2, 1))
        q0 = pl.multiple_of(i * SW_BLOCK, SW_BLOCK)
        k0 = pl.multiple_of(start, SW_BLOCK)
        qb = q_ref[0, pl.ds(q0, SW_BLOCK), :]
        lhs = jnp.concatenate([qb[:, jj * LANES:(jj + 1) * LANES] for jj in range(SW_GROUP)], axis=0)
        k = k_ref[0, pl.ds(k0, SW_KT), :]
        v = v_ref[0, pl.ds(k0, SW_KT), :]
        pv, sink_e = [], []
        for g, sel in enumerate((low, high)):
            kg = jnp.where(sel, k, jnp.zeros_like(k))
            s = lax.dot_general(lhs, kg, (((1,), (1,)), ((), ())),
                                preferred_element_type=jnp.float32)
            s = s + bias_ref[variant, g]
            es, ss = [], []
            for jj in range(SW_GROUP):
                sj = s[jj * SW_BLOCK:(jj + 1) * SW_BLOCK]
                sink = sink_ref[g * SW_GROUP + jj]
                mj = jnp.maximum(jnp.max(sj, axis=-1, keepdims=True), sink)
                es.append(jnp.exp2((sj - mj).astype(jnp.bfloat16)))
                ss.append(jnp.exp2(sink - mj))
            e = jnp.concatenate(es, axis=0)
            sink_e.append(jnp.concatenate(ss, axis=0))
            vg = jnp.where(sel, v, jnp.ones_like(v))
            pv.append(jnp.dot(e, vg, preferred_element_type=jnp.float32))
        val, den = _normalise_pair(pv[0], pv[1], low)
        den = den + jnp.where(low, sink_e[0], sink_e[1])
        o = val / den
        o = jnp.concatenate([o[jj * SW_BLOCK:(jj + 1) * SW_BLOCK] for jj in range(SW_GROUP)], axis=1)
        z = z_ref[0, pl.ds(q0, SW_BLOCK), :].astype(jnp.float32)
        o_ref[0, pl.ds(q0, SW_BLOCK), :] = (o * (z * jax.nn.sigmoid(z))).astype(jnp.bfloat16)
        return carry

    lax.fori_loop(0, nblk, body, 0, unroll=2)


def _sw_attn(proj, bias, sink):
    b, s, _ = proj.shape
    return pl.pallas_call(
        functools.partial(_sw_kernel, seq=s),
        grid=(b, s // SW_CHUNK),
        in_specs=[pl.BlockSpec(memory_space=pltpu.SMEM),
                  pl.BlockSpec((1, SW_CHUNK, SW_WIDTH), lambda i, c: (i, c, OFF_QB // SW_WIDTH)),
                  pl.BlockSpec((1, s, LANES), lambda i, c: (i, 0, OFF_KB // LANES)),
                  pl.BlockSpec((1, s, LANES), lambda i, c: (i, 0, OFF_VB // LANES)),
                  pl.BlockSpec((1, SW_CHUNK, SW_WIDTH), lambda i, c: (i, c, OFF_ZB // SW_WIDTH)),
                  pl.BlockSpec(bias.shape, lambda i, c: (0, 0, 0, 0))],
        out_specs=pl.BlockSpec((1, SW_CHUNK, SW_WIDTH), lambda i, c: (i, c, 0)),
        out_shape=jax.ShapeDtypeStruct((b, s, SW_WIDTH), jnp.bfloat16),
        compiler_params=pltpu.CompilerParams(
            dimension_semantics=("parallel", "parallel"), vmem_limit_bytes=VMEM_LIMIT),
        name="sw_attn",
    )(sink, proj, proj, proj, proj, bias)


def _out_kernel(x_ref, oa_ref, ob_ref, ga_ref, gb_ref, gate_ref, g_ref, wpa_ref, wpb_ref, wout_ref, o_ref):
    a = jnp.dot(oa_ref[0], wpa_ref[...], preferred_element_type=jnp.float32)
    bb = jnp.dot(ob_ref[0], wpb_ref[...], preferred_element_type=jnp.float32)
    merged = (jax.nn.sigmoid(ga_ref[0].astype(jnp.float32)) * a
              + jax.nn.sigmoid(gb_ref[0].astype(jnp.float32)) * bb)
    y = jnp.dot(merged.astype(jnp.bfloat16), wout_ref[...], preferred_element_type=jnp.float32)
    ms = jnp.mean(y * y, axis=-1, keepdims=True)
    yn = y * lax.rsqrt(ms + RMS_EPS) * g_ref[...]
    o_ref[0] = x_ref[0] + gate_ref[0] * yn


def _out_proj(x, oa, ob, proj, gate, g_post, w_pa, w_pb, w_out):
    b, s, d = x.shape
    tok = lambda i, j: (i, j, 0)
    const = lambda i, j: (0, 0)
    return pl.pallas_call(
        _out_kernel,
        grid=(b, s // TM_OUT),
        in_specs=[pl.BlockSpec((1, TM_OUT, d), tok),
                  pl.BlockSpec((1, TM_OUT, NA_WIDTH), tok),
                  pl.BlockSpec((1, TM_OUT, SW_WIDTH), tok),
                  pl.BlockSpec((1, TM_OUT, d), lambda i, j: (i, j, OFF_GA // D_MODEL)),
                  pl.BlockSpec((1, TM_OUT, d), lambda i, j: (i, j, OFF_GB // D_MODEL)),
                  pl.BlockSpec((1, 1, d), lambda i, j: (i, 0, 0)),
                  pl.BlockSpec((1, d), const),
                  pl.BlockSpec((NA_WIDTH, d), const),
                  pl.BlockSpec((SW_WIDTH, d), const),
                  pl.BlockSpec((d, d), const)],
        out_specs=pl.BlockSpec((1, TM_OUT, d), tok),
        out_shape=jax.ShapeDtypeStruct((b, s, d), jnp.float32),
        compiler_params=pltpu.CompilerParams(
            dimension_semantics=("parallel", "parallel"), vmem_limit_bytes=VMEM_LIMIT),
        name="out_proj",
    )(x, oa, ob, proj, proj, gate, g_post, w_pa, w_pb, w_out)


def _encoder_layer(xs, adas, g_pre, g_post, w_in, na_bias, sw_bias, sw_sink, w_pa, w_pb, w_out):
    outs = []
    for x, ada in zip(xs, adas):
        shift, scale, gate = jnp.split(ada[:, None, :], 3, axis=-1)
        proj = _in_proj(x, scale, shift, g_pre, w_in)
        oa = _na_attn(proj, na_bias)
        ob = _sw_attn(proj, sw_bias, sw_sink)
        outs.append(_out_proj(x, oa, ob, proj, gate, g_post, w_pa, w_pb, w_out))
    return outs


def kernel(x_prompt, x_sample, c_prompt, c_sample, w_ada, b_ada, g_pre, g_post, w_in, na_rpb, sw_sink,
           w_pa, w_pb, w_out):
    depth = w_in.shape[0]
    seq = x_prompt.shape[1]
    assert seq % (GRID_W * NA_QROWS) == 0 and seq % SW_CHUNK == 0
    assert x_sample.shape[1] == seq and seq // GRID_W >= 2 * NA_KROWS
    xs = [x_prompt, x_sample]
    nbs = [x_prompt.shape[0], x_sample.shape[0]]
    c_all = jnp.concatenate([c_prompt, c_sample], axis=0)
    sw_bias = jnp.asarray(_sw_bias_tables())
    bf16 = jnp.bfloat16
    for l in range(depth):
        ada = _ada(c_all, w_ada[l], b_ada[l])
        adas = [ada[:nbs[0]], ada[nbs[0]:]]
        w_in_l = (w_in[l][:, IN_COL_PERM] * IN_COL_SCALE).astype(bf16)
        w_pb_l = w_pb[l][SW_HEAD_PERM].astype(bf16)
        na_bias = _na_bias_tables(na_rpb[l], seq // GRID_W)
        sink = sw_sink[l].astype(jnp.float32) * LOG2E
        xs = _encoder_layer(xs, adas, g_pre[l].reshape(1, -1), g_post[l].reshape(1, -1), w_in_l, na_bias,
                            sw_bias, sink, w_pa[l].astype(bf16), w_pb_l, w_out[l].astype(bf16))
    return (xs[0], xs[1])
```

```python
import functools
import math

import jax
import jax.numpy as jnp
import numpy as np
from jax import lax
from jax.experimental import pallas as pl
from jax.experimental.pallas import tpu as pltpu

D_MODEL = 1024
GRID_W = 64
HEAD_DIM = 64
NA_HEADS = 8
NA_WIDTH = NA_HEADS * HEAD_DIM
NA_KH = 8
NA_KH_MAX = 8
NA_KW = 16
SW_HEADS = 8
SW_KV_HEADS = 2
SW_GROUP = SW_HEADS // SW_KV_HEADS
SW_WIDTH = SW_HEADS * HEAD_DIM
SW_KV_WIDTH = SW_KV_HEADS * HEAD_DIM
SW_WINDOW = 128
SW_BLOCK = 128
RMS_EPS = 1e-6
D_IN = 4 * NA_WIDTH + 2 * SW_WIDTH + 2 * SW_KV_WIDTH + 2 * D_MODEL

LANES = 128
NEG = -1e30
LOG2E = math.log2(math.e)
VMEM_LIMIT = 56 * 1024 * 1024

OFF_GA, OFF_GB = 0, D_MODEL
OFF_QA = 2 * D_MODEL
OFF_KA = OFF_QA + NA_WIDTH
OFF_VA = OFF_KA + NA_WIDTH
OFF_ZA = OFF_VA + NA_WIDTH
OFF_QB = OFF_ZA + NA_WIDTH
OFF_ZB = OFF_QB + SW_WIDTH
OFF_KB = OFF_ZB + SW_WIDTH
OFF_VB = OFF_KB + SW_KV_WIDTH
SW_HEAD_ORDER = (0, 4, 1, 5, 2, 6, 3, 7)

NA_QROWS = 4
NA_KROWS = 12
NA_QT = NA_QROWS * GRID_W
NA_KT = NA_KROWS * GRID_W
SW_KT = 3 * SW_BLOCK

TM_IN = 512
CN_IN = 768
TM_OUT = 1024
SW_CHUNK = 2048


def _in_col_perm():
    o = np.cumsum((0, NA_WIDTH, NA_WIDTH, NA_WIDTH, NA_WIDTH, SW_WIDTH, SW_KV_WIDTH, SW_KV_WIDTH, SW_WIDTH,
                   D_MODEL))
    qa, ka, va, za, qb, kb, vb, zb, ga, gb = (int(v) for v in o)
    ar = np.arange
    heads = np.concatenate([h * HEAD_DIM + ar(HEAD_DIM) for h in SW_HEAD_ORDER])
    perm = np.concatenate([
        ga + ar(D_MODEL), gb + ar(D_MODEL),
        qa + ar(NA_WIDTH), ka + ar(NA_WIDTH), va + ar(NA_WIDTH), za + ar(NA_WIDTH),
        qb + heads, zb + heads, kb + ar(SW_KV_WIDTH), vb + ar(SW_KV_WIDTH)])
    col_scale = np.ones((D_IN,), np.float32)
    col_scale[OFF_QA:OFF_QA + NA_WIDTH] = HEAD_DIM ** -0.5 * LOG2E
    col_scale[OFF_QB:OFF_QB + SW_WIDTH] = HEAD_DIM ** -0.5 * LOG2E
    return perm, col_scale, heads


IN_COL_PERM, IN_COL_SCALE, SW_HEAD_PERM = _in_col_perm()


def _take_runs(w, perm, axis):
    cuts = [0] + [i for i in range(1, len(perm)) if perm[i] != perm[i - 1] + 1] + [len(perm)]
    runs = [lax.slice_in_dim(w, int(perm[a]), int(perm[b - 1]) + 1, axis=axis) for a, b in zip(cuts, cuts[1:])]
    return jnp.concatenate(runs, axis=axis)


def _low_lanes():
    return lax.broadcasted_iota(jnp.int32, (1, LANES), 1) < HEAD_DIM


def _ada_kernel(c_ref, w_ref, b_ref, o_ref):
    c = c_ref[...]
    sc = c * jax.nn.sigmoid(c)
    o_ref[...] = jnp.dot(sc, w_ref[...], preferred_element_type=jnp.float32,
                         precision=lax.Precision.HIGHEST) + b_ref[...]


def _ada(c, w_ada, b_ada):
    nb, d = c.shape
    n = w_ada.shape[1]
    tn = 1024
    return pl.pallas_call(
        _ada_kernel,
        grid=(n // tn,),
        in_specs=[pl.BlockSpec((nb, d), lambda j: (0, 0)),
                  pl.BlockSpec((d, tn), lambda j: (0, j)),
                  pl.BlockSpec((1, tn), lambda j: (0, j))],
        out_specs=pl.BlockSpec((nb, tn), lambda j: (0, j)),
        out_shape=jax.ShapeDtypeStruct((nb, n), jnp.float32),
        name="ada",
    )(c, w_ada, b_ada.reshape(1, n))


def _in_proj_kernel(x_ref, scale_ref, shift_ref, g_ref, w_ref, o_ref):
    x = x_ref[0]
    ms = jnp.mean(x * x, axis=-1, keepdims=True)
    y = x * lax.rsqrt(ms + RMS_EPS) * g_ref[...]
    h = (y * (1.0 + scale_ref[0]) + shift_ref[0]).astype(jnp.bfloat16)
    for c in range(D_IN // CN_IN):
        cols = slice(c * CN_IN, (c + 1) * CN_IN)
        o_ref[0, :, cols] = jnp.dot(h, w_ref[:, cols],
                                    preferred_element_type=jnp.float32).astype(jnp.bfloat16)


def _in_proj(x, scale, shift, g_pre, w_in):
    b, s, d = x.shape
    return pl.pallas_call(
        _in_proj_kernel,
        grid=(b, s // TM_IN),
        in_specs=[pl.BlockSpec((1, TM_IN, d), lambda i, j: (i, j, 0)),
                  pl.BlockSpec((1, 1, d), lambda i, j: (i, 0, 0)),
                  pl.BlockSpec((1, 1, d), lambda i, j: (i, 0, 0)),
                  pl.BlockSpec((1, d), lambda i, j: (0, 0)),
                  pl.BlockSpec((d, D_IN), lambda i, j: (0, 0))],
        out_specs=pl.BlockSpec((1, TM_IN, D_IN), lambda i, j: (i, j, 0)),
        out_shape=jax.ShapeDtypeStruct((b, s, D_IN), jnp.bfloat16),
        compiler_params=pltpu.CompilerParams(
            dimension_semantics=("parallel", "parallel"), vmem_limit_bytes=VMEM_LIMIT),
        name="in_proj",
    )(x, scale, shift, g_pre, w_in)


def _normalise_pair(pv_low, pv_high, low):
    val = jnp.where(low, pv_low, pv_high)
    den = pltpu.roll(jnp.where(low, pv_high, pv_low), HEAD_DIM, axis=1)
    return val, den


def _na_bias_tables(rpb, rows):
    pad = NA_QROWS
    c = np.arange(GRID_W)
    c0 = np.clip(c - NA_KW // 2, 0, GRID_W - NA_KW)
    col_ok = (c[None, :] >= c0[:, None]) & (c[None, :] < c0[:, None] + NA_KW)
    cpad = GRID_W - NA_KW
    rpb = jnp.pad(rpb.astype(jnp.float32) * LOG2E, ((0, 0), (0, 0), (cpad, cpad)))
    t = jnp.stack([rpb[:, :, GRID_W - 1 - ci:2 * GRID_W - 1 - ci] for ci in range(GRID_W)], axis=1)
    t = jnp.where(col_ok[None, :, None, :], t, NEG)
    t = jnp.pad(t, ((0, 0), (0, 0), (pad, pad), (0, 0)), constant_values=NEG)
    j = np.arange(NA_KROWS)
    blocks = []
    for r, ks in ((0, 0), (NA_QROWS, 0), (rows - NA_QROWS, rows - NA_KROWS)):
        strips = []
        for a in range(NA_QROWS):
            rq = r + a
            r0 = min(max(rq - NA_KH // 2, 0), rows - NA_KH)
            row_ok = (ks + j >= r0) & (ks + j < r0 + NA_KH)
            d0 = ks - rq + NA_KH_MAX - 1 + pad
            assert 0 <= d0 and d0 + NA_KROWS <= t.shape[2]
            strip = jnp.where(row_ok[None, None, :, None], t[:, :, d0:d0 + NA_KROWS, :], NEG)
            strips.append(strip.reshape(NA_HEADS, GRID_W, NA_KT))
        blocks.append(jnp.concatenate(strips, axis=1))
    t = jnp.stack(blocks, axis=1)
    return t.reshape(NA_HEADS // 2, 2, 3, NA_QT, NA_KT).transpose(0, 2, 1, 3, 4)


def _na_kernel(q_ref, k_ref, v_ref, z_ref, bias_ref, o_ref, x0_ref, x1_ref, *, rows):
    nblk = rows // NA_QROWS
    low = _low_lanes()
    high = jnp.logical_not(low)

    def key_start(rb):
        ks = jnp.clip(rb * NA_QROWS - NA_KH // 2, 0, rows - NA_KROWS)
        return pl.multiple_of(ks * GRID_W, GRID_W)

    def scores(rb, x_ref):
        variant = jnp.where(rb == 0, 0, jnp.where(rb == nblk - 1, 2, 1))
        q = q_ref[0, pl.ds(pl.multiple_of(rb * NA_QT, NA_QT), NA_QT), :]
        k = k_ref[0, pl.ds(key_start(rb), NA_KT), :]
        for hh, sel in enumerate((low, high)):
            qm = jnp.where(sel, q, jnp.zeros_like(q))
            s = lax.dot_general(qm, k, (((1,), (1,)), ((), ())),
                                preferred_element_type=jnp.float32)
            s = s + bias_ref[0, variant, hh]
            m = jnp.max(s, axis=-1, keepdims=True)
            x_ref[hh] = (s - m).astype(jnp.bfloat16)

    def values(rb, x_ref):
        q0 = pl.multiple_of(rb * NA_QT, NA_QT)
        v = v_ref[0, pl.ds(key_start(rb), NA_KT), :]
        pv = []
        for hh, sel in enumerate((low, high)):
            vh = jnp.where(sel, v, jnp.ones_like(v))
            pv.append(jnp.dot(jnp.exp2(x_ref[hh]), vh, preferred_element_type=jnp.float32))
        val, den = _normalise_pair(pv[0], pv[1], low)
        z = z_ref[0, pl.ds(q0, NA_QT), :].astype(jnp.float32)
        o_ref[0, pl.ds(q0, NA_QT), :] = (val / den * (z * jax.nn.sigmoid(z))).astype(jnp.bfloat16)

    scores(0, x0_ref)

    def body(i, carry):
        rb = 2 * i
        scores(rb + 1, x1_ref)
        values(rb, x0_ref)
        scores(jnp.minimum(rb + 2, nblk - 1), x0_ref)
        values(rb + 1, x1_ref)
        return carry

    lax.fori_loop(0, nblk // 2, body, 0, unroll=4)


def _na_attn(proj, bias):
    b, s, _ = proj.shape
    rows = s // GRID_W
    npair = NA_HEADS // 2

    def col(off):
        return lambda i, p: (i, 0, off // LANES + p)

    return pl.pallas_call(
        functools.partial(_na_kernel, rows=rows),
        grid=(b, npair),
        in_specs=[pl.BlockSpec((1, s, LANES), col(OFF_QA)),
                  pl.BlockSpec((1, s, LANES), col(OFF_KA)),
                  pl.BlockSpec((1, s, LANES), col(OFF_VA)),
                  pl.BlockSpec((1, s, LANES), col(OFF_ZA)),
                  pl.BlockSpec((1, 3, 2, NA_QT, NA_KT), lambda i, p: (p, 0, 0, 0, 0))],
        out_specs=pl.BlockSpec((1, s, LANES), lambda i, p: (i, 0, p)),
        out_shape=jax.ShapeDtypeStruct((b, s, NA_WIDTH), jnp.bfloat16),
        scratch_shapes=[pltpu.VMEM((2, NA_QT, NA_KT), jnp.bfloat16),
                        pltpu.VMEM((2, NA_QT, NA_KT), jnp.bfloat16)],
        compiler_params=pltpu.CompilerParams(
            dimension_semantics=("parallel", "parallel"), vmem_limit_bytes=VMEM_LIMIT),
        name="na_attn",
    )(proj, proj, proj, proj, bias)


def _sw_bias_tables():
    slopes = (2.0 ** (-8.0 * (np.arange(SW_HEADS) + 1) / SW_HEADS)).astype(np.float32)
    i = np.arange(SW_BLOCK)
    col = np.arange(SW_KT)
    out = np.empty((3, SW_KV_HEADS, SW_GROUP, SW_BLOCK, SW_KT), np.float32)
    for var in range(3):
        dist = np.abs(col[None, :] - var * SW_BLOCK - i[:, None])
        for g in range(SW_KV_HEADS):
            for jj in range(SW_GROUP):
                bias = -(slopes[g * SW_GROUP + jj] * dist.astype(np.float32)) * np.float32(LOG2E)
                out[var, g, jj] = np.where(dist <= SW_WINDOW, bias, NEG)
    return out.reshape(3, SW_KV_HEADS, SW_GROUP * SW_BLOCK, SW_KT)


def _sw_kernel(sink_ref, q_ref, k_ref, v_ref, z_ref, bias_ref, o_ref, *, seq):
    nblk_total = seq // SW_BLOCK
    nblk = SW_CHUNK // SW_BLOCK
    chunk = pl.program_id(1)
    low = _low_lanes()
    high = jnp.logical_not(low)

    def body(i, carry):
        nb = chunk * nblk + i
        start = jnp.clip((nb - 1) * SW_BLOCK, 0, seq - SW_KT)
        variant = jnp.where(nb == 0, 0, jnp.where(nb == nblk_total - 1, 2, 1))
        q0 = pl.multiple_of(i * SW_BLOCK, SW_BLOCK)
        k0 = pl.multiple_of(start, SW_BLOCK)
        qb = q_ref[0, pl.ds(q0, SW_BLOCK), :]
        lhs = jnp.concatenate([qb[:, jj * LANES:(jj + 1) * LANES] for jj in range(SW_GROUP)], axis=0)
        k = k_ref[0, pl.ds(k0, SW_KT), :]
        v = v_ref[0, pl.ds(k0, SW_KT), :]
        pv, sink_e = [], []
        for g, sel in enumerate((low, high)):
            kg = jnp.where(sel, k, jnp.zeros_like(k))
            s = lax.dot_general(lhs, kg, (((1,), (1,)), ((), ())),
                                preferred_element_type=jnp.float32)
            s = s + bias_ref[variant, g]
            es, ss = [], []
            for jj in range(SW_GROUP):
                sj = s[jj * SW_BLOCK:(jj + 1) * SW_BLOCK]
                sink = sink_ref[g * SW_GROUP + jj]
                mj = jnp.maximum(jnp.max(sj, axis=-1, keepdims=True), sink)
                es.append(jnp.exp2((sj - mj).astype(jnp.bfloat16)))
                ss.append(jnp.exp2(sink - mj))
            e = jnp.concatenate(es, axis=0)
            sink_e.append(jnp.concatenate(ss, axis=0))
            vg = jnp.where(sel, v, jnp.ones_like(v))
            pv.append(jnp.dot(e, vg, preferred_element_type=jnp.float32))
        val, den = _normalise_pair(pv[0], pv[1], low)
        den = den + jnp.where(low, sink_e[0], sink_e[1])
        o = val / den
        o = jnp.concatenate([o[jj * SW_BLOCK:(jj + 1) * SW_BLOCK] for jj in range(SW_GROUP)], axis=1)
        z = z_ref[0, pl.ds(q0, SW_BLOCK), :].astype(jnp.float32)
        o_ref[0, pl.ds(q0, SW_BLOCK), :] = (o * (z * jax.nn.sigmoid(z))).astype(jnp.bfloat16)
        return carry

    lax.fori_loop(0, nblk, body, 0, unroll=2)


def _sw_attn(proj, bias, sink):
    b, s, _ = proj.shape
    return pl.pallas_call(
        functools.partial(_sw_kernel, seq=s),
        grid=(b, s // SW_CHUNK),
        in_specs=[pl.BlockSpec(memory_space=pltpu.SMEM),
                  pl.BlockSpec((1, SW_CHUNK, SW_WIDTH), lambda i, c: (i, c, OFF_QB // SW_WIDTH)),
                  pl.BlockSpec((1, s, LANES), lambda i, c: (i, 0, OFF_KB // LANES)),
                  pl.BlockSpec((1, s, LANES), lambda i, c: (i, 0, OFF_VB // LANES)),
                  pl.BlockSpec((1, SW_CHUNK, SW_WIDTH), lambda i, c: (i, c, OFF_ZB // SW_WIDTH)),
                  pl.BlockSpec(bias.shape, lambda i, c: (0, 0, 0, 0))],
        out_specs=pl.BlockSpec((1, SW_CHUNK, SW_WIDTH), lambda i, c: (i, c, 0)),
        out_shape=jax.ShapeDtypeStruct((b, s, SW_WIDTH), jnp.bfloat16),
        compiler_params=pltpu.CompilerParams(
            dimension_semantics=("parallel", "parallel"), vmem_limit_bytes=VMEM_LIMIT),
        name="sw_attn",
    )(sink, proj, proj, proj, proj, bias)


def _out_kernel(x_ref, oa_ref, ob_ref, ga_ref, gb_ref, gate_ref, g_ref, wpa_ref, wpb_ref, wout_ref, o_ref):
    a = jnp.dot(oa_ref[0], wpa_ref[...], preferred_element_type=jnp.float32)
    bb = jnp.dot(ob_ref[0], wpb_ref[...], preferred_element_type=jnp.float32)
    merged = (jax.nn.sigmoid(ga_ref[0].astype(jnp.float32)) * a
              + jax.nn.sigmoid(gb_ref[0].astype(jnp.float32)) * bb)
    y = jnp.dot(merged.astype(jnp.bfloat16), wout_ref[...], preferred_element_type=jnp.float32)
    ms = jnp.mean(y * y, axis=-1, keepdims=True)
    yn = y * lax.rsqrt(ms + RMS_EPS) * g_ref[...]
    o_ref[0] = x_ref[0] + gate_ref[0] * yn


def _out_proj(x, oa, ob, proj, gate, g_post, w_pa, w_pb, w_out):
    b, s, d = x.shape
    tok = lambda i, j: (i, j, 0)
    const = lambda i, j: (0, 0)
    return pl.pallas_call(
        _out_kernel,
        grid=(b, s // TM_OUT),
        in_specs=[pl.BlockSpec((1, TM_OUT, d), tok),
                  pl.BlockSpec((1, TM_OUT, NA_WIDTH), tok),
                  pl.BlockSpec((1, TM_OUT, SW_WIDTH), tok),
                  pl.BlockSpec((1, TM_OUT, d), lambda i, j: (i, j, OFF_GA // D_MODEL)),
                  pl.BlockSpec((1, TM_OUT, d), lambda i, j: (i, j, OFF_GB // D_MODEL)),
                  pl.BlockSpec((1, 1, d), lambda i, j: (i, 0, 0)),
                  pl.BlockSpec((1, d), const),
                  pl.BlockSpec((NA_WIDTH, d), const),
                  pl.BlockSpec((SW_WIDTH, d), const),
                  pl.BlockSpec((d, d), const)],
        out_specs=pl.BlockSpec((1, TM_OUT, d), tok),
        out_shape=jax.ShapeDtypeStruct((b, s, d), jnp.float32),
        compiler_params=pltpu.CompilerParams(
            dimension_semantics=("parallel", "parallel"), vmem_limit_bytes=VMEM_LIMIT),
        name="out_proj",
    )(x, oa, ob, proj, proj, gate, g_post, w_pa, w_pb, w_out)


def _encoder_layer(xs, adas, g_pre, g_post, w_in, na_bias, sw_bias, sw_sink, w_pa, w_pb, w_out):
    outs = []
    for x, ada in zip(xs, adas):
        shift, scale, gate = jnp.split(ada[:, None, :], 3, axis=-1)
        proj = _in_proj(x, scale, shift, g_pre, w_in)
        oa = _na_attn(proj, na_bias)
        ob = _sw_attn(proj, sw_bias, sw_sink)
        outs.append(_out_proj(x, oa, ob, proj, gate, g_post, w_pa, w_pb, w_out))
    return outs


def kernel(x_prompt, x_sample, c_prompt, c_sample, w_ada, b_ada, g_pre, g_post, w_in, na_rpb, sw_sink,
           w_pa, w_pb, w_out):
    depth = w_in.shape[0]
    seq = x_prompt.shape[1]
    assert seq % (GRID_W * NA_QROWS) == 0 and seq % SW_CHUNK == 0
    assert x_sample.shape[1] == seq and seq // GRID_W >= 2 * NA_KROWS
    xs = [x_prompt, x_sample]
    nbs = [x_prompt.shape[0], x_sample.shape[0]]
    c_all = jnp.concatenate([c_prompt, c_sample], axis=0)
    sw_bias = jnp.asarray(_sw_bias_tables())
    bf16 = jnp.bfloat16
    for l in range(depth):
        ada = _ada(c_all, w_ada[l], b_ada[l])
        adas = [ada[:nbs[0]], ada[nbs[0]:]]
        w_in_l = (_take_runs(w_in[l], IN_COL_PERM, axis=1) * IN_COL_SCALE).astype(bf16)
        w_pb_l = _take_runs(w_pb[l], SW_HEAD_PERM, axis=0).astype(bf16)
        na_bias = _na_bias_tables(na_rpb[l], seq // GRID_W)
        sink = sw_sink[l].astype(jnp.float32) * LOG2E
        xs = _encoder_layer(xs, adas, g_pre[l].reshape(1, -1), g_post[l].reshape(1, -1), w_in_l, na_bias,
                            sw_bias, sink, w_pa[l].astype(bf16), w_pb_l, w_out[l].astype(bf16))
    return (xs[0], xs[1])
```
